```python
import math
import jax, jax.numpy as jnp
from jax import lax
import numpy as np

D_MODEL = 4096
BATCH = 1
SEQ = 16384
DEPTH = 4

CHUNK = 64
N_EVEN = (DEPTH + 1) // 2
N_ODD = DEPTH // 2
RET_HEADS = 8
RET_WIDTH = D_MODEL // 2
RET_DV = RET_WIDTH // RET_HEADS
RET_DK = RET_DV // 2
ROPE_BASE = 10000.0
S5_WIDTH = D_MODEL - RET_WIDTH
S5_GROUP = 16
S5_GROUPS = S5_WIDTH // S5_GROUP
S5_STATE = 64
EVEN_IN = 2 * RET_HEADS * RET_DK + 2 * RET_WIDTH + S5_WIDTH
GDN_HEADS = 16
GDN_DV = D_MODEL // GDN_HEADS
GDN_DK = GDN_DV // 2
CONV_K = 4
ODD_IN = 2 * GDN_HEADS * GDN_DK + 2 * GDN_HEADS * GDN_DV + 2 * GDN_HEADS
N_EXPERTS = 32
TOP_K = 4
EXPERT_FF = D_MODEL // 16
SWIGLU_LIMIT = 7.0
SWIGLU_ALPHA = 1.702
DEEPNORM_ALPHA = (2 * DEPTH) ** 0.25
DEEPNORM_BETA = (8 * DEPTH) ** -0.25
LN_EPS = 1e-5
NORM_EPS = 1e-6

kernel_name = 'hybrid_retention_s5_gdn_moe_deepnorm'


def _layer_norm(x, g, b):
    xf = x.astype(jnp.float32)
    mu = jnp.mean(xf, axis=-1, keepdims=True)
    var = jnp.mean(jnp.square(xf - mu), axis=-1, keepdims=True)
    y = (xf - mu) * lax.rsqrt(var + LN_EPS) * g.astype(jnp.float32) + b.astype(jnp.float32)
    return y.astype(x.dtype)


def _head_layer_norm(t):
    mu = jnp.mean(t, axis=-1, keepdims=True)
    var = jnp.mean(jnp.square(t - mu), axis=-1, keepdims=True)
    return (t - mu) * lax.rsqrt(var + NORM_EPS)


def _rms_norm(t):
    return t * lax.rsqrt(jnp.mean(jnp.square(t), axis=-1, keepdims=True) + NORM_EPS)


def _l2_norm(t):
    return t * lax.rsqrt(jnp.sum(jnp.square(t), axis=-1, keepdims=True) + NORM_EPS)


def _to_chunks(t):
    b, l, h, d = t.shape
    return t.reshape(b, l // CHUNK, CHUNK, h, d).transpose(1, 0, 3, 2, 4)


def _from_chunks(t):
    n, b, h, c, d = t.shape
    return t.transpose(1, 0, 3, 2, 4).reshape(b, n * c, h, d)


def _rotary(t, pos):
    half = t.shape[-1] // 2
    inv_freq = jnp.power(ROPE_BASE, -jnp.arange(half, dtype=jnp.float32) / half)
    ang = pos[:, None] * inv_freq[None, :]
    cos = jnp.cos(ang)[None, :, None, :]
    sin = jnp.sin(ang)[None, :, None, :]
    t1, t2 = t[..., :half], t[..., half:]
    return jnp.concatenate([t1 * cos - t2 * sin, t1 * sin + t2 * cos], axis=-1)


def _causal_conv(x, w):
    k, c = w.shape
    return lax.conv_general_dilated(x, w[:, None, :], window_strides=(1,), padding=[(k - 1, 0)],
                                    dimension_numbers=('NWC', 'WIO', 'NWC'), feature_group_count=c)


def _retention(q, k, v):
    b, l, h, dk = q.shape
    dv = v.shape[-1]
    log_gamma = jnp.log1p(-jnp.exp2(-5.0 - jnp.arange(h, dtype=jnp.float32)))
    pos = jnp.arange(CHUNK, dtype=jnp.float32)
    dist = jnp.abs(pos[:, None] - pos[None, :])
    d_intra = jnp.exp(log_gamma[:, None, None] * dist)
    q_dec = jnp.exp(log_gamma[:, None] * (pos + 1.0))[:, :, None]
    k_dec = jnp.exp(log_gamma[:, None] * (CHUNK - 1.0 - pos))[:, :, None]
    chunk_dec = jnp.exp(log_gamma * CHUNK)[:, None, None]
    qc, kc, vc = _to_chunks(q), _to_chunks(k), _to_chunks(v)
    scores = jnp.einsum('nbhqd,nbhkd->nbhqk', qc, kc) * d_intra
    o_intra = jnp.einsum('nbhqk,nbhkv->nbhqv', scores, vc)

    def step(state, inp):
        q_i, k_i, v_i = inp
        o_cross = jnp.einsum('bhqd,bhdv->bhqv', q_i, state) * q_dec
        state = state * chunk_dec + jnp.einsum('bhkd,bhkv->bhdv', k_i * k_dec, v_i)
        return state, o_cross

    s0 = jnp.zeros((b, h, dk, dv), jnp.float32)
    _, o_cross = lax.scan(step, s0, (qc, kc, vc))
    return _from_chunks(o_intra + o_cross)


def _complex_affine_combine(left, right):
    a1r, a1i, b1r, b1i = left
    a2r, a2i, b2r, b2i = right
    return (a1r * a2r - a1i * a2i, a1r * a2i + a1i * a2r,
            a2r * b1r - a2i * b1i + b2r, a2r * b1i + a2i * b1r + b2i)


def _s5(u, b_re, b_im, c_re, c_im, d, a_re, a_im, log_dt):
    bsz, l, _ = u.shape
    dt = jnp.exp(log_dt)[:, None]
    lam_re = jnp.minimum(a_re, -1e-4)
    lam_im = a_im
    mag = jnp.exp(lam_re * dt)
    ab_re = mag * jnp.cos(lam_im * dt)
    ab_im = mag * jnp.sin(lam_im * dt)
    den = jnp.square(lam_re) + jnp.square(lam_im)
    f_re = ((ab_re - 1.0) * lam_re + ab_im * lam_im) / den
    f_im = (ab_im * lam_re - (ab_re - 1.0) * lam_im) / den
    bb_re = f_re[..., None] * b_re - f_im[..., None] * b_im
    bb_im = f_re[..., None] * b_im + f_im[..., None] * b_re
    shape = (bsz, CHUNK, S5_GROUPS, S5_STATE)
    a_re_c = jnp.broadcast_to(ab_re, shape)
    a_im_c = jnp.broadcast_to(ab_im, shape)
    uc = u.reshape(bsz, l // CHUNK, CHUNK, S5_GROUPS, S5_GROUP).transpose(1, 0, 2, 3, 4)

    def step(carry, u_i):
        h_re, h_im = carry
        bu_re = jnp.einsum('bcgi,gpi->bcgp', u_i, bb_re)
        bu_im = jnp.einsum('bcgi,gpi->bcgp', u_i, bb_im)
        p_re, p_im, x_re, x_im = lax.associative_scan(
            _complex_affine_combine, (a_re_c, a_im_c, bu_re, bu_im), axis=1)
        x_re, x_im = (x_re + p_re * h_re[:, None] - p_im * h_im[:, None],
                      x_im + p_re * h_im[:, None] + p_im * h_re[:, None])
        y = jnp.einsum('bcgp,gip->bcgi', x_re, c_re) - jnp.einsum('bcgp,gip->bcgi', x_im, c_im)
        return (x_re[:, -1], x_im[:, -1]), y

    h0 = jnp.zeros((bsz, S5_GROUPS, S5_STATE), jnp.float32)
    _, y = lax.scan(step, (h0, h0), uc)
    y = y.transpose(1, 0, 2, 3, 4).reshape(bsz, l, S5_WIDTH)
    return y + d * u


def _gated_delta(q, k, v, g, beta):
    b, l, h, dk = q.shape
    dv = v.shape[-1]
    qc, kc, vc = _to_chunks(q), _to_chunks(k), _to_chunks(v)
    gcum = jnp.cumsum(_to_chunks(g[..., None])[..., 0], axis=-1)
    bc = _to_chunks(beta[..., None])
    causal = jnp.tril(jnp.ones((CHUNK, CHUNK), dtype=bool))
    strict = jnp.tril(jnp.ones((CHUNK, CHUNK), dtype=bool), k=-1)
    decay = jnp.exp(jnp.where(causal, gcum[..., :, None] - gcum[..., None, :], -jnp.inf))
    k_beta = kc * bc
    v_beta = vc * bc
    lmat = jnp.where(strict, jnp.einsum('nbhid,nbhjd->nbhij', k_beta, kc) * decay, 0.0)
    t_mat = lmat + jnp.eye(CHUNK, dtype=jnp.float32)
    rhs = jnp.concatenate([v_beta, k_beta * jnp.exp(gcum)[..., None]], axis=-1)
    sol = lax.linalg.triangular_solve(t_mat, rhs, left_side=True, lower=True, unit_diagonal=True)
    w_v, k_cum = sol[..., :dv], sol[..., dv:]
    attn = jnp.where(causal, jnp.einsum('nbhid,nbhjd->nbhij', qc, kc) * decay, 0.0)

    def step(state, inp):
        q_i, k_i, wv_i, kcum_i, g_i, a_i = inp
        v_new = wv_i - jnp.einsum('bhcd,bhdv->bhcv', kcum_i, state)
        o = (jnp.einsum('bhcd,bhdv->bhcv', q_i * jnp.exp(g_i)[..., None], state)
             + jnp.einsum('bhij,bhjv->bhiv', a_i, v_new))
        g_last = g_i[..., -1]
        state = (state * jnp.exp(g_last)[..., None, None]
                 + jnp.einsum('bhcd,bhcv->bhdv', k_i * jnp.exp(g_last[..., None] - g_i)[..., None], v_new))
        return state, o

    s0 = jnp.zeros((b, h, dk, dv), jnp.float32)
    _, o = lax.scan(step, s0, (qc, kc, w_v, k_cum, gcum, attn))
    return _from_chunks(o)


def _even_mixer(x, w_in, w_out, s5_b_re, s5_b_im, s5_c_re, s5_c_im, s5_d, s5_a_re, s5_a_im,
                s5_log_dt, s5_w_glu, s5_b_glu):
    f32 = jnp.float32
    bsz, l, _ = x.shape
    h = jnp.einsum('bld,de->ble', x, w_in).astype(f32)
    qk = RET_HEADS * RET_DK
    q, k, v, gate, u = jnp.split(h, [qk, 2 * qk, 2 * qk + RET_WIDTH, 2 * qk + 2 * RET_WIDTH], axis=-1)
    pos = jnp.arange(l, dtype=f32)
    q = _rotary(q.reshape(bsz, l, RET_HEADS, RET_DK), pos)
    k = _rotary(k.reshape(bsz, l, RET_HEADS, RET_DK), pos) * (RET_DK ** -0.5)
    o_ret = _head_layer_norm(_retention(q, k, v.reshape(bsz, l, RET_HEADS, RET_DV)))
    o_ret = o_ret.reshape(bsz, l, RET_WIDTH) * jax.nn.silu(gate)
    y = _s5(u, s5_b_re.astype(f32), s5_b_im.astype(f32), s5_c_re.astype(f32), s5_c_im.astype(f32),
            s5_d.astype(f32), s5_a_re.astype(f32), s5_a_im.astype(f32), s5_log_dt.astype(f32))
    y = jax.nn.gelu(y)
    y = y * jax.nn.sigmoid(jnp.einsum('blc,ce->ble', y, s5_w_glu.astype(f32)) + s5_b_glu.astype(f32))
    mixed = jnp.concatenate([o_ret, y], axis=-1).astype(x.dtype)
    return jnp.einsum('blc,cd->bld', mixed, w_out)


def _odd_mixer(x, w_in, w_out, conv_w, a_log, dt_bias, norm_w):
    f32 = jnp.float32
    bsz, l, _ = x.shape
    h = jnp.einsum('bld,de->ble', x, w_in).astype(f32)
    qkw = GDN_HEADS * GDN_DK
    vw = GDN_HEADS * GDN_DV
    qkv, gate, a_raw, b_raw = jnp.split(h, [2 * qkw + vw, 2 * qkw + 2 * vw, 2 * qkw + 2 * vw + GDN_HEADS], axis=-1)
    qkv = jax.nn.silu(_causal_conv(qkv, conv_w.astype(f32)))
    q, k, v = jnp.split(qkv, [qkw, 2 * qkw], axis=-1)
    q = _l2_norm(q.reshape(bsz, l, GDN_HEADS, GDN_DK)) * (GDN_DK ** -0.5)
    k = _l2_norm(k.reshape(bsz, l, GDN_HEADS, GDN_DK))
    v = v.reshape(bsz, l, GDN_HEADS, GDN_DV)
    g = -jnp.exp(a_log.astype(f32)) * jax.nn.softplus(a_raw + dt_bias.astype(f32))
    beta = jax.nn.sigmoid(b_raw)
    o = _gated_delta(q, k, v, g, beta)
    o = _rms_norm(o) * norm_w.astype(f32) * jax.nn.silu(gate.reshape(bsz, l, GDN_HEADS, GDN_DV))
    return jnp.einsum('blc,cd->bld', o.reshape(bsz, l, vw).astype(x.dtype), w_out)


def _moe(x, w_router, b_router, w_gate, b_gate, w_up, b_up, w_down, b_down):
    bsz, l, dm = x.shape
    xt = x.reshape(bsz * l, dm)
    logits = (xt @ w_router + b_router).astype(jnp.float32)
    top_val, top_idx = lax.top_k(logits, TOP_K)
    probs = jax.nn.softmax(top_val, axis=-1)
    gates = jnp.sum(jax.nn.one_hot(top_idx, N_EXPERTS, dtype=jnp.float32) * probs[..., None], axis=1)
    out = jnp.zeros((bsz * l, dm), jnp.float32)
    for e in range(N_EXPERTS):
        glin = jnp.minimum((xt @ w_gate[e] + b_gate[e]).astype(jnp.float32), SWIGLU_LIMIT)
        ulin = jnp.clip((xt @ w_up[e] + b_up[e]).astype(jnp.float32), -SWIGLU_LIMIT, SWIGLU_LIMIT)
        act = (ulin + 1.0) * glin * jax.nn.sigmoid(SWIGLU_ALPHA * glin)
        y_e = act.astype(x.dtype) @ w_down[e] + b_down[e]
        out = out + gates[:, e:e + 1] * y_e.astype(jnp.float32)
    return out.reshape(bsz, l, dm).astype(x.dtype)


def setup_inputs(seed: int = 0) -> dict:
    key = jax.random.key(seed)
    ks = iter(jax.random.split(key, 40))
    f32 = jnp.float32

    def nrm(shape, scale):
        return jax.random.normal(next(ks), shape, f32) * scale

    def unif(shape, lo, hi):
        return jax.random.uniform(next(ks), shape, f32, lo, hi)

    ne, no, nl = N_EVEN, N_ODD, DEPTH
    G, P, I = S5_GROUPS, S5_STATE, S5_GROUP
    x = nrm((BATCH, SEQ, D_MODEL), 1.0)
    ev_w_in = nrm((ne, D_MODEL, EVEN_IN), D_MODEL ** -0.5)
    ev_w_out = nrm((ne, RET_WIDTH + S5_WIDTH, D_MODEL), DEEPNORM_BETA * (RET_WIDTH + S5_WIDTH) ** -0.5)
    ev_s5_b_re = nrm((ne, G, P, I), (2 * I) ** -0.5)
    ev_s5_b_im = nrm((ne, G, P, I), (2 * I) ** -0.5)
    ev_s5_c_re = nrm((ne, G, I, P), (2 * P) ** -0.5)
    ev_s5_c_im = nrm((ne, G, I, P), (2 * P) ** -0.5)
    ev_s5_d = nrm((ne, S5_WIDTH), 1.0)
    ev_s5_a_re = -0.5 + nrm((ne, G, P), 0.01)
    ev_s5_a_im = math.pi * jnp.arange(P, dtype=f32) + nrm((ne, G, P), 0.01)
    ev_s5_log_dt = unif((ne, G), math.log(1e-3), math.log(1e-1))
    ev_s5_w_glu = nrm((ne, S5_WIDTH, S5_WIDTH), S5_WIDTH ** -0.5)
    ev_s5_b_glu = nrm((ne, S5_WIDTH), 0.01)
    ev_ln_g = 1.0 + nrm((ne, D_MODEL), 0.02)
    ev_ln_b = nrm((ne, D_MODEL), 0.02)
    od_w_in = nrm((no, D_MODEL, ODD_IN), D_MODEL ** -0.5)
    od_w_out = nrm((no, GDN_HEADS * GDN_DV, D_MODEL), DEEPNORM_BETA * (GDN_HEADS * GDN_DV) ** -0.5)
    od_conv_w = nrm((no, CONV_K, 2 * GDN_HEADS * GDN_DK + GDN_HEADS * GDN_DV), CONV_K ** -0.5)
    od_a_log = jnp.log(unif((no, GDN_HEADS), 1.0, 16.0))
    dt = jnp.exp(unif((no, GDN_HEADS), math.log(1e-3), math.log(1e-1)))
    od_dt_bias = dt + jnp.log(-jnp.expm1(-dt))
    od_norm_w = 1.0 + nrm((no, GDN_DV), 0.02)
    od_ln_g = 1.0 + nrm((no, D_MODEL), 0.02)
    od_ln_b = nrm((no, D_MODEL), 0.02)
    moe_w_router = nrm((nl, D_MODEL, N_EXPERTS), D_MODEL ** -0.5)
    moe_b_router = nrm((nl, N_EXPERTS), 0.01)
    moe_w_gate = nrm((nl, N_EXPERTS, D_MODEL, EXPERT_FF), D_MODEL ** -0.5)
    moe_b_gate = nrm((nl, N_EXPERTS, EXPERT_FF), 0.01)
    moe_w_up = nrm((nl, N_EXPERTS, D_MODEL, EXPERT_FF), D_MODEL ** -0.5)
    moe_b_up = nrm((nl, N_EXPERTS, EXPERT_FF), 0.01)
    moe_w_down = nrm((nl, N_EXPERTS, EXPERT_FF, D_MODEL), DEEPNORM_BETA * EXPERT_FF ** -0.5)
    moe_b_down = nrm((nl, N_EXPERTS, D_MODEL), 0.01)
    moe_ln_g = 1.0 + nrm((nl, D_MODEL), 0.02)
    moe_ln_b = nrm((nl, D_MODEL), 0.02)
    return {'x': x,
            'ev_w_in': ev_w_in, 'ev_w_out': ev_w_out,
            'ev_s5_b_re': ev_s5_b_re, 'ev_s5_b_im': ev_s5_b_im,
            'ev_s5_c_re': ev_s5_c_re, 'ev_s5_c_im': ev_s5_c_im, 'ev_s5_d': ev_s5_d,
            'ev_s5_a_re': ev_s5_a_re, 'ev_s5_a_im': ev_s5_a_im, 'ev_s5_log_dt': ev_s5_log_dt,
            'ev_s5_w_glu': ev_s5_w_glu, 'ev_s5_b_glu': ev_s5_b_glu,
            'ev_ln_g': ev_ln_g, 'ev_ln_b': ev_ln_b,
            'od_w_in': od_w_in, 'od_w_out': od_w_out, 'od_conv_w': od_conv_w,
            'od_a_log': od_a_log, 'od_dt_bias': od_dt_bias, 'od_norm_w': od_norm_w,
            'od_ln_g': od_ln_g, 'od_ln_b': od_ln_b,
            'moe_w_router': moe_w_router, 'moe_b_router': moe_b_router,
            'moe_w_gate': moe_w_gate, 'moe_b_gate': moe_b_gate,
            'moe_w_up': moe_w_up, 'moe_b_up': moe_b_up,
            'moe_w_down': moe_w_down, 'moe_b_down': moe_b_down,
            'moe_ln_g': moe_ln_g, 'moe_ln_b': moe_ln_b}


def reference(x, ev_w_in, ev_w_out, ev_s5_b_re, ev_s5_b_im, ev_s5_c_re, ev_s5_c_im, ev_s5_d,
              ev_s5_a_re, ev_s5_a_im, ev_s5_log_dt, ev_s5_w_glu, ev_s5_b_glu, ev_ln_g, ev_ln_b,
              od_w_in, od_w_out, od_conv_w, od_a_log, od_dt_bias, od_norm_w, od_ln_g, od_ln_b,
              moe_w_router, moe_b_router, moe_w_gate, moe_b_gate, moe_w_up, moe_b_up,
              moe_w_down, moe_b_down, moe_ln_g, moe_ln_b):
    h = x
    for layer in range(DEPTH):
        p = layer // 2
        if layer % 2 == 0:
            mix = _even_mixer(h, ev_w_in[p], ev_w_out[p], ev_s5_b_re[p], ev_s5_b_im[p], ev_s5_c_re[p],
                              ev_s5_c_im[p], ev_s5_d[p], ev_s5_a_re[p], ev_s5_a_im[p], ev_s5_log_dt[p],
                              ev_s5_w_glu[p], ev_s5_b_glu[p])
            h = _layer_norm(DEEPNORM_ALPHA * h + mix, ev_ln_g[p], ev_ln_b[p])
        else:
            mix = _odd_mixer(h, od_w_in[p], od_w_out[p], od_conv_w[p], od_a_log[p], od_dt_bias[p], od_norm_w[p])
            h = _layer_norm(DEEPNORM_ALPHA * h + mix, od_ln_g[p], od_ln_b[p])
        ff = _moe(h, moe_w_router[layer], moe_b_router[layer], moe_w_gate[layer], moe_b_gate[layer],
                  moe_w_up[layer], moe_b_up[layer], moe_w_down[layer], moe_b_down[layer])
        h = _layer_norm(DEEPNORM_ALPHA * h + ff, moe_ln_g[layer], moe_ln_b[layer])
    return h
```

```python
import functools
import math

import jax
import jax.numpy as jnp
from jax import lax
from jax.experimental import pallas as pl
from jax.experimental.pallas import tpu as pltpu

F32 = jnp.float32
BF16 = jnp.bfloat16
I32 = jnp.int32
HIGHEST = lax.Precision.HIGHEST

D_MODEL = 4096
DEPTH = 4
CHUNK = 64
RET_HEADS = 8
RET_WIDTH = D_MODEL // 2
RET_DV = RET_WIDTH // RET_HEADS
RET_DK = RET_DV // 2
ROPE_BASE = 10000.0
S5_WIDTH = D_MODEL - RET_WIDTH
S5_GROUP = 16
S5_GROUPS = S5_WIDTH // S5_GROUP
S5_STATE = 64
GDN_HEADS = 16
GDN_DV = D_MODEL // GDN_HEADS
GDN_DK = GDN_DV // 2
CONV_K = 4
N_EXPERTS = 32
TOP_K = 4
EXPERT_FF = D_MODEL // 16
SWIGLU_LIMIT = 7.0
SWIGLU_ALPHA = 1.702
DEEPNORM_ALPHA = (2 * DEPTH) ** 0.25
LN_EPS = 1e-5
NORM_EPS = 1e-6

V7X_VMEM_LIMIT_BYTES = 56 * 1024 * 1024
LANES = 128
S5_SUB = 16
RET_BLOCK = 256
GDN_BLOCK = 256
MOE_TM = 256
MOE_TB = 128
NEG_BIG = -3.0e38


def _cparams(sem, vmem=V7X_VMEM_LIMIT_BYTES):
    return pltpu.CompilerParams(dimension_semantics=sem, vmem_limit_bytes=vmem)


def _dot(a, b):
    return jnp.dot(a, b, preferred_element_type=F32)


def _dot_nt(a, b):
    return lax.dot_general(a, b, (((1,), (1,)), ((), ())), preferred_element_type=F32)


def _dot_tn(a, b):
    return lax.dot_general(a, b, (((0,), (0,)), ((), ())), preferred_element_type=F32)


def _dot_hi(a, b):
    return jnp.dot(a, b, preferred_element_type=F32, precision=HIGHEST)


def _sigmoid(x):
    return 1.0 / (1.0 + jnp.exp(-x))


def _mm_kernel(x_ref, w_ref, o_ref):
    o_ref[...] = _dot(x_ref[...].astype(BF16), w_ref[...]).astype(o_ref.dtype)


def _matmul(x, w, out_dtype, tm, tn):
    m, k = x.shape
    n = w.shape[1]
    tm, tn = min(tm, m), min(tn, n)
    return pl.pallas_call(
        _mm_kernel,
        grid=(m // tm, n // tn),
        in_specs=[pl.BlockSpec((tm, k), lambda i, j: (i, 0)),
                  pl.BlockSpec((k, tn), lambda i, j: (0, j))],
        out_specs=pl.BlockSpec((tm, tn), lambda i, j: (i, j)),
        out_shape=jax.ShapeDtypeStruct((m, n), out_dtype),
        compiler_params=_cparams(("parallel", "parallel")),
        name="proj_matmul",
    )(x, w)


def _layer_norm_rows(z, g, b):
    mu = jnp.mean(z, axis=-1, keepdims=True)
    zc = z - mu
    var = jnp.mean(zc * zc, axis=-1, keepdims=True)
    return zc * lax.rsqrt(var + LN_EPS) * g + b


def _mm_ln_kernel(x_ref, w_ref, res_ref, g_ref, b_ref, o_ref, obf_ref, acc_ref, *, nk):
    k = pl.program_id(1)

    @pl.when(k == 0)
    def _():
        acc_ref[...] = jnp.zeros_like(acc_ref)

    acc_ref[...] += _dot(x_ref[...].astype(BF16), w_ref[...])

    @pl.when(k == nk - 1)
    def _():
        y = _layer_norm_rows(DEEPNORM_ALPHA * res_ref[...] + acc_ref[...], g_ref[...], b_ref[...])
        o_ref[...] = y
        obf_ref[...] = y.astype(BF16)


def _matmul_ln(x, w, res, g, b, tm=256, tk=512):
    m, kdim = x.shape
    n = w.shape[1]
    tm, tk = min(tm, m), min(tk, kdim)
    nk = kdim // tk
    return pl.pallas_call(
        functools.partial(_mm_ln_kernel, nk=nk),
        grid=(m // tm, nk),
        in_specs=[pl.BlockSpec((tm, tk), lambda i, k: (i, k)),
                  pl.BlockSpec((tk, n), lambda i, k: (k, 0)),
                  pl.BlockSpec((tm, n), lambda i, k: (i, 0)),
                  pl.BlockSpec((1, n), lambda i, k: (0, 0)),
                  pl.BlockSpec((1, n), lambda i, k: (0, 0))],
        out_specs=[pl.BlockSpec((tm, n), lambda i, k: (i, 0)),
                   pl.BlockSpec((tm, n), lambda i, k: (i, 0))],
        out_shape=[jax.ShapeDtypeStruct((m, n), F32), jax.ShapeDtypeStruct((m, n), BF16)],
        scratch_shapes=[pltpu.VMEM((tm, n), F32)],
        compiler_params=_cparams(("parallel", "arbitrary")),
        name="proj_layernorm",
    )(x, w, res, g.reshape(1, n), b.reshape(1, n))


def _mm_glu_kernel(x_ref, yt_ref, w_ref, b_ref, o_ref):
    z = _dot(x_ref[...], w_ref[...]) + b_ref[...]
    o_ref[...] = (yt_ref[...].astype(F32) * _sigmoid(z)).astype(o_ref.dtype)


def _matmul_glu(y, w, b, tm=1024, tn=512):
    m, k = y.shape
    n = w.shape[1]
    tm, tn = min(tm, m), min(tn, n)
    return pl.pallas_call(
        _mm_glu_kernel,
        grid=(m // tm, n // tn),
        in_specs=[pl.BlockSpec((tm, k), lambda i, j: (i, 0)),
                  pl.BlockSpec((tm, tn), lambda i, j: (i, j)),
                  pl.BlockSpec((k, tn), lambda i, j: (0, j)),
                  pl.BlockSpec((1, tn), lambda i, j: (0, j))],
        out_specs=pl.BlockSpec((tm, tn), lambda i, j: (i, j)),
        out_shape=jax.ShapeDtypeStruct((m, n), BF16),
        compiler_params=_cparams(("parallel", "parallel")),
        name="s5_glu",
    )(y, y, w, b.reshape(1, n))


def _ret_kernel(bdec_ref, q_ref, k_ref, v_ref, gate_ref, cos_ref, sin_ref, dmask_ref, qdec_ref,
                kdec_ref, o_ref, s_ref):
    h = pl.program_id(0)

    @pl.when(pl.program_id(1) == 0)
    def _():
        s_ref[...] = jnp.zeros_like(s_ref)

    cos = cos_ref[...]
    sin = sin_ref[...]
    q = q_ref[...].astype(F32)
    k = k_ref[...].astype(F32)
    half = RET_DK // 2
    qr = q * cos + pltpu.roll(q, half, 1) * sin
    kr = (k * cos + pltpu.roll(k, half, 1) * sin) * (RET_DK ** -0.5)
    v = v_ref[...].astype(BF16)
    scores = _dot_nt(qr.astype(BF16), kr.astype(BF16)) * dmask_ref[0]
    o = _dot(scores.astype(BF16), v)
    state = s_ref[...]
    o = o + _dot((qr * qdec_ref[0]).astype(BF16), state.astype(BF16))
    s_ref[...] = state * bdec_ref[h] + _dot_tn((kr * kdec_ref[0]).astype(BF16), v)
    mu = jnp.mean(o, axis=-1, keepdims=True)
    oc = o - mu
    var = jnp.mean(oc * oc, axis=-1, keepdims=True)
    on = oc * lax.rsqrt(var + NORM_EPS)
    gate = gate_ref[...].astype(F32)
    o_ref[...] = (on * (gate * _sigmoid(gate))).astype(o_ref.dtype)


def _retention_tables(t, bt):
    half = RET_DK // 2
    inv_freq = jnp.power(ROPE_BASE, -jnp.arange(half, dtype=F32) / half)
    ang = jnp.arange(t, dtype=F32)[:, None] * inv_freq[None, :]
    cos, sin = jnp.cos(ang), jnp.sin(ang)
    cos2 = jnp.concatenate([cos, cos], axis=-1)
    sin2 = jnp.concatenate([-sin, sin], axis=-1)
    log_gamma = jnp.log1p(-jnp.exp2(-5.0 - jnp.arange(RET_HEADS, dtype=F32)))
    pos = jnp.arange(bt, dtype=F32)
    dist = jnp.abs(pos[:, None] - pos[None, :])
    chunk_id = jnp.arange(bt) // CHUNK
    visible = chunk_id[None, :] <= chunk_id[:, None]
    dmask = jnp.where(visible[None], jnp.exp(log_gamma[:, None, None] * dist[None]), 0.0)
    qdec = jnp.exp(log_gamma[:, None] * (pos + 1.0))
    kdec = jnp.exp(log_gamma[:, None] * (bt - 1.0 - pos))
    qdec = jnp.broadcast_to(qdec[:, :, None], (RET_HEADS, bt, RET_DK))
    kdec = jnp.broadcast_to(kdec[:, :, None], (RET_HEADS, bt, RET_DK))
    bdec = jnp.exp(log_gamma * bt)
    return cos2, sin2, dmask, qdec, kdec, bdec


def _retention(p, t):
    bt = min(RET_BLOCK, t)
    cos2, sin2, dmask, qdec, kdec, bdec = _retention_tables(t, bt)
    kq = RET_HEADS
    kv = 2 * RET_HEADS * RET_DK // RET_DV
    kg = kv + RET_HEADS
    grid_spec = pltpu.PrefetchScalarGridSpec(
        num_scalar_prefetch=0,
        grid=(RET_HEADS, t // bt),
        in_specs=[pl.BlockSpec(memory_space=pltpu.SMEM),
                  pl.BlockSpec((bt, RET_DK), lambda h, i: (i, h)),
                  pl.BlockSpec((bt, RET_DK), lambda h, i: (i, kq + h)),
                  pl.BlockSpec((bt, RET_DV), lambda h, i: (i, kv + h)),
                  pl.BlockSpec((bt, RET_DV), lambda h, i: (i, kg + h)),
                  pl.BlockSpec((bt, RET_DK), lambda h, i: (i, 0)),
                  pl.BlockSpec((bt, RET_DK), lambda h, i: (i, 0)),
                  pl.BlockSpec((1, bt, bt), lambda h, i: (h, 0, 0)),
                  pl.BlockSpec((1, bt, RET_DK), lambda h, i: (h, 0, 0)),
                  pl.BlockSpec((1, bt, RET_DK), lambda h, i: (h, 0, 0))],
        out_specs=pl.BlockSpec((bt, RET_DV), lambda h, i: (i, h)),
        scratch_shapes=[pltpu.VMEM((RET_DK, RET_DV), F32)])
    return pl.pallas_call(
        _ret_kernel,
        grid_spec=grid_spec,
        out_shape=jax.ShapeDtypeStruct((t, RET_WIDTH), BF16),
        compiler_params=_cparams(("parallel", "arbitrary")),
        name="retention",
    )(bdec, p, p, p, p, cos2, sin2, dmask, qdec, kdec)


def _gelu_tanh(x):
    c = math.sqrt(2.0 / math.pi)
    return 0.5 * x * (1.0 + jnp.tanh(c * (x + 0.044715 * (x * x * x))))


def _s5_kernel(u_ref, m_ref, bm_ref, cm_ref, pw_ref, d_ref, y_ref, *, nlev):
    u = u_ref[0].astype(F32)
    ub = u.astype(BF16)
    nc = u.shape[0]
    hstate = _dot(ub, bm_ref[0])
    row = lax.broadcasted_iota(I32, hstate.shape, 0)
    pw = pw_ref[0]
    for lev in range(nlev):
        s = 1 << lev
        hs = jnp.where(row >= s, pltpu.roll(hstate, s, 0), 0.0)
        hstate = (hstate + pw[2 * lev:2 * lev + 1, :] * hs
                  + pw[2 * lev + 1:2 * lev + 2, :] * pltpu.roll(hs, S5_STATE, 1))
    if nc > 1:
        hprev = jnp.where(row >= 1, pltpu.roll(hstate, 1, 0), 0.0)
    else:
        hprev = jnp.zeros_like(hstate)
    y = _dot(ub, m_ref[0]) + _dot(hprev.astype(BF16), cm_ref[0]) + d_ref[0] * u
    y_ref[0] = _gelu_tanh(y).astype(y_ref.dtype)


def _s5_tables(b_re, b_im, c_re, c_im, d, a_re, a_im, log_dt, nc):
    g, p, i = S5_GROUPS, S5_STATE, S5_GROUP
    l = S5_SUB
    dt = jnp.exp(log_dt)[:, None]
    lam_re = jnp.minimum(a_re, -1e-4)
    lam_im = a_im
    mag = jnp.exp(lam_re * dt)
    ab_re = mag * jnp.cos(lam_im * dt)
    ab_im = mag * jnp.sin(lam_im * dt)
    den = jnp.square(lam_re) + jnp.square(lam_im)
    f_re = ((ab_re - 1.0) * lam_re + ab_im * lam_im) / den
    f_im = (ab_im * lam_re - (ab_re - 1.0) * lam_im) / den
    bb_re = f_re[..., None] * b_re - f_im[..., None] * b_im
    bb_im = f_re[..., None] * b_im + f_im[..., None] * b_re

    def apow(n):
        n = jnp.asarray(n, F32)[..., None, None]
        m = jnp.exp(n * (lam_re * dt))
        return m * jnp.cos(n * (lam_im * dt)), m * jnp.sin(n * (lam_im * dt))

    wr, wi = apow(jnp.arange(l))
    xr = wr[..., None] * bb_re[None] - wi[..., None] * bb_im[None]
    xi = wr[..., None] * bb_im[None] + wi[..., None] * bb_re[None]
    kern = (jnp.einsum('gip,jgpk->gjik', c_re, xr, precision=HIGHEST)
            - jnp.einsum('gip,jgpk->gjik', c_im, xi, precision=HIGHEST))
    lag = jnp.arange(l)[None, :] - jnp.arange(l)[:, None]
    toe = kern[:, jnp.clip(lag, 0, l - 1)]
    toe = jnp.where((lag >= 0)[None, :, :, None, None], toe, 0.0)
    m_mat = toe.transpose(0, 1, 4, 2, 3).reshape(g, l * i, l * i)
    wr, wi = apow(l - 1 - jnp.arange(l))
    br = wr[..., None] * bb_re[None] - wi[..., None] * bb_im[None]
    bi = wr[..., None] * bb_im[None] + wi[..., None] * bb_re[None]
    bm = jnp.concatenate([br, bi], axis=2)
    bm = bm.transpose(1, 0, 3, 2).reshape(g, l * i, 2 * p)
    wr, wi = apow(jnp.arange(l) + 1.0)
    cr = (c_re.transpose(0, 2, 1)[None] * wr[..., None]
          - c_im.transpose(0, 2, 1)[None] * wi[..., None])
    ci = (-c_re.transpose(0, 2, 1)[None] * wi[..., None]
          - c_im.transpose(0, 2, 1)[None] * wr[..., None])
    cm = jnp.concatenate([cr, ci], axis=2)
    cm = cm.transpose(1, 2, 0, 3).reshape(g, 2 * p, l * i)
    nlev = max(1, int(math.ceil(math.log2(nc)))) if nc > 1 else 0
    if nlev:
        wr, wi = apow(float(l) * (2.0 ** jnp.arange(nlev)))
        pw = jnp.stack([jnp.concatenate([wr, wr], axis=-1),
                        jnp.concatenate([-wi, wi], axis=-1)], axis=1)
        pw = pw.reshape(2 * nlev, g, 2 * p).transpose(1, 0, 2)
    else:
        pw = jnp.zeros((g, 2, 2 * p), F32)
    dd = jnp.tile(d.reshape(g, 1, i), (1, l, 1)).reshape(g, 1, l * i)
    return m_mat.astype(BF16), bm.astype(BF16), cm.astype(BF16), pw, dd, nlev


def _s5(u, b_re, b_im, c_re, c_im, d, a_re, a_im, log_dt):
    t = u.shape[0]
    g, l, i = S5_GROUPS, S5_SUB, S5_GROUP
    nc = t // l
    m_mat, bm, cm, pw, dd, nlev = _s5_tables(b_re, b_im, c_re, c_im, d, a_re, a_im, log_dt, nc)
    ug = u.reshape(nc, l, g, i).transpose(2, 0, 1, 3).reshape(g, nc, l * i)
    npw = pw.shape[1]
    y = pl.pallas_call(
        functools.partial(_s5_kernel, nlev=nlev),
        grid=(g,),
        in_specs=[pl.BlockSpec((1, nc, l * i), lambda j: (j, 0, 0)),
                  pl.BlockSpec((1, l * i, l * i), lambda j: (j, 0, 0)),
                  pl.BlockSpec((1, l * i, 2 * S5_STATE), lambda j: (j, 0, 0)),
                  pl.BlockSpec((1, 2 * S5_STATE, l * i), lambda j: (j, 0, 0)),
                  pl.BlockSpec((1, npw, 2 * S5_STATE), lambda j: (j, 0, 0)),
                  pl.BlockSpec((1, 1, l * i), lambda j: (j, 0, 0))],
        out_specs=pl.BlockSpec((1, nc, l * i), lambda j: (j, 0, 0)),
        out_shape=jax.ShapeDtypeStruct((g, nc, l * i), BF16),
        compiler_params=_cparams(("parallel",)),
        name="s5_scan",
    )(ug, m_mat, bm, cm, pw, dd)
    return y.reshape(g, nc, l, i).transpose(1, 2, 0, 3).reshape(t, g * i)


def _causal_conv_silu(x, xprev, w):
    row = lax.broadcasted_iota(I32, x.shape, 0)
    y = w[CONV_K - 1:CONV_K, :] * x
    for j in range(1, CONV_K):
        shifted = jnp.where(row >= j, pltpu.roll(x, j, 0), pltpu.roll(xprev, j, 0))
        y = y + w[CONV_K - 1 - j:CONV_K - j, :] * shifted
    return y * _sigmoid(y)


def _softplus(x):
    return jnp.maximum(x, 0.0) + jnp.log(1.0 + jnp.exp(-jnp.abs(x)))


def _gdn_kernel(alog_ref, dtb_ref, q_ref, k_ref, v_ref, gate_ref, wq_ref, wk_ref, wv_ref, ab_ref,
                abt_ref, nw_ref, o_ref, s_ref, qp_ref, kp_ref, vp_ref):
    h = pl.program_id(0)

    @pl.when(pl.program_id(1) == 0)
    def _():
        s_ref[...] = jnp.zeros_like(s_ref)
        qp_ref[...] = jnp.zeros_like(qp_ref)
        kp_ref[...] = jnp.zeros_like(kp_ref)
        vp_ref[...] = jnp.zeros_like(vp_ref)

    q_raw = q_ref[...].astype(F32)
    k_raw = k_ref[...].astype(F32)
    v_raw = v_ref[...].astype(F32)
    q = _causal_conv_silu(q_raw, qp_ref[...], wq_ref[...])
    k = _causal_conv_silu(k_raw, kp_ref[...], wk_ref[...])
    v = _causal_conv_silu(v_raw, vp_ref[...], wv_ref[...])
    qp_ref[...] = q_raw
    kp_ref[...] = k_raw
    vp_ref[...] = v_raw
    q = q * lax.rsqrt(jnp.sum(q * q, axis=-1, keepdims=True) + NORM_EPS) * (GDN_DK ** -0.5)
    k = k * lax.rsqrt(jnp.sum(k * k, axis=-1, keepdims=True) + NORM_EPS)

    bt = q.shape[0]
    neg_a = alog_ref[h]
    dtb = dtb_ref[h]
    ab = ab_ref[...]
    lane = lax.broadcasted_iota(I32, ab.shape, 1)
    a_col = jnp.sum(jnp.where(lane == h, ab, 0.0), axis=1, keepdims=True)
    b_col = jnp.sum(jnp.where(lane == GDN_HEADS + h, ab, 0.0), axis=1, keepdims=True)
    abt = abt_ref[...]
    sub = lax.broadcasted_iota(I32, abt.shape, 0)
    a_row = jnp.sum(jnp.where(sub == h, abt, 0.0), axis=0, keepdims=True)
    g_col = neg_a * _softplus(a_col + dtb)
    g_row = neg_a * _softplus(a_row + dtb)
    beta = _sigmoid(b_col)

    ci = lax.broadcasted_iota(I32, (CHUNK, CHUNK), 0)
    cj = lax.broadcasted_iota(I32, (CHUNK, CHUNK), 1)
    lower = ci >= cj
    strict = ci > cj
    eye = (ci == cj).astype(F32)
    gate = gate_ref[...].astype(F32)
    nw = nw_ref[...]
    state = s_ref[...]
    for c in range(bt // CHUNK):
        sl = slice(c * CHUNK, (c + 1) * CHUNK)
        qc, kc, vc = q[sl], k[sl], v[sl]
        bc = beta[sl]
        gcol, grow = g_col[sl], g_row[:, sl]
        gcum_col = jnp.sum(jnp.where(lower, grow, 0.0), axis=1, keepdims=True)
        gcum_row = jnp.sum(jnp.where(ci <= cj, gcol, 0.0), axis=0, keepdims=True)
        decay = jnp.where(lower, jnp.exp(jnp.where(lower, gcum_col - gcum_row, 0.0)), 0.0)
        k_beta = kc * bc
        v_beta = vc * bc
        kb16 = kc.astype(BF16)
        lmat = jnp.where(strict, _dot_nt(k_beta.astype(BF16), kb16) * decay, 0.0)
        tinv = eye - lmat
        pw = _dot_hi(lmat, lmat)
        n = 2
        while True:
            tinv = tinv + _dot_hi(tinv, pw)
            n *= 2
            if n >= CHUNK:
                break
            pw = _dot_hi(pw, pw)
        eg = jnp.exp(gcum_col)
        w_v = _dot_hi(tinv, v_beta)
        k_cum = _dot_hi(tinv, k_beta * eg)
        attn = jnp.where(lower, _dot_nt(qc.astype(BF16), kb16) * decay, 0.0)
        g_last = gcum_col[CHUNK - 1:CHUNK, :]
        s16 = state.astype(BF16)
        v_new = w_v - _dot(k_cum.astype(BF16), s16)
        vn16 = v_new.astype(BF16)
        o = _dot((qc * eg).astype(BF16), s16) + _dot(attn.astype(BF16), vn16)
        kd = kc * jnp.exp(g_last - gcum_col)
        state = state * jnp.exp(g_last) + _dot_tn(kd.astype(BF16), vn16)
        rms = lax.rsqrt(jnp.mean(o * o, axis=-1, keepdims=True) + NORM_EPS)
        gc = gate[sl]
        o_ref[sl, :] = (o * rms * nw * (gc * _sigmoid(gc))).astype(o_ref.dtype)
    s_ref[...] = state


def _gated_deltanet(p, ab, conv_w, a_log, dt_bias, norm_w, t):
    bt = min(GDN_BLOCK, t)
    hq = GDN_HEADS
    kk = hq
    kv = 2 * hq * GDN_DK // GDN_DV
    kg = kv + hq
    abt = ab.T
    grid_spec = pltpu.PrefetchScalarGridSpec(
        num_scalar_prefetch=0,
        grid=(hq, t // bt),
        in_specs=[pl.BlockSpec(memory_space=pltpu.SMEM),
                  pl.BlockSpec(memory_space=pltpu.SMEM),
                  pl.BlockSpec((bt, GDN_DK), lambda h, i: (i, h)),
                  pl.BlockSpec((bt, GDN_DK), lambda h, i: (i, kk + h)),
                  pl.BlockSpec((bt, GDN_DV), lambda h, i: (i, kv + h)),
                  pl.BlockSpec((bt, GDN_DV), lambda h, i: (i, kg + h)),
                  pl.BlockSpec((CONV_K, GDN_DK), lambda h, i: (0, h)),
                  pl.BlockSpec((CONV_K, GDN_DK), lambda h, i: (0, kk + h)),
                  pl.BlockSpec((CONV_K, GDN_DV), lambda h, i: (0, kv + h)),
                  pl.BlockSpec((bt, 2 * hq), lambda h, i: (i, 0)),
                  pl.BlockSpec((2 * hq, bt), lambda h, i: (0, i)),
                  pl.BlockSpec((1, GDN_DV), lambda h, i: (0, 0))],
        out_specs=pl.BlockSpec((bt, GDN_DV), lambda h, i: (i, h)),
        scratch_shapes=[pltpu.VMEM((GDN_DK, GDN_DV), F32),
                        pltpu.VMEM((bt, GDN_DK), F32),
                        pltpu.VMEM((bt, GDN_DK), F32),
                        pltpu.VMEM((bt, GDN_DV), F32)])
    return pl.pallas_call(
        _gdn_kernel,
        grid_spec=grid_spec,
        out_shape=jax.ShapeDtypeStruct((t, hq * GDN_DV), BF16),
        compiler_params=_cparams(("parallel", "arbitrary")),
        name="gated_deltanet",
    )(-jnp.exp(a_log), dt_bias, p, p, p, p, conv_w, conv_w, conv_w, ab, abt, norm_w.reshape(1, GDN_DV))


def _router_kernel(x_ref, w_ref, b_ref, idx_ref, prob_ref, pre_ref, cnt_ref, run_ref):
    @pl.when(pl.program_id(0) == 0)
    def _():
        run_ref[...] = jnp.zeros_like(run_ref)

    logits = _dot_hi(x_ref[...], w_ref[...]) + b_ref[...]
    tm = logits.shape[0]
    lane = lax.broadcasted_iota(I32, logits.shape, 1)
    work = logits
    vals, sels, hots = [], [], []
    for _ in range(TOP_K):
        m = jnp.max(work, axis=1, keepdims=True)
        sel = jnp.min(jnp.where(work == m, lane, N_EXPERTS), axis=1, keepdims=True)
        hot = lane == sel
        vals.append(m)
        sels.append(sel)
        hots.append(hot)
        work = jnp.where(hot, NEG_BIG, work)
    exps = [jnp.exp(vk - vals[0]) for vk in vals]
    denom = exps[0] + exps[1] + exps[2] + exps[3]
    cnt = jnp.zeros(logits.shape, F32)
    for hot in hots:
        cnt = cnt + hot.astype(F32)
    ri = lax.broadcasted_iota(I32, (tm, tm), 0)
    rj = lax.broadcasted_iota(I32, (tm, tm), 1)
    tri = jnp.where(ri > rj, 1.0, 0.0).astype(BF16)
    pexcl = _dot(tri, cnt.astype(BF16)) + run_ref[...]
    lane_o = lax.broadcasted_iota(I32, (tm, LANES), 1)
    idx_o = jnp.zeros((tm, LANES), I32)
    prob_o = jnp.zeros((tm, LANES), F32)
    pre_o = jnp.zeros((tm, LANES), I32)
    for kk in range(TOP_K):
        pre_k = jnp.sum(jnp.where(hots[kk], pexcl, 0.0), axis=1, keepdims=True).astype(I32)
        idx_o = jnp.where(lane_o == kk, sels[kk], idx_o)
        prob_o = jnp.where(lane_o == kk, exps[kk] / denom, prob_o)
        pre_o = jnp.where(lane_o == kk, pre_k, pre_o)
    idx_ref[...] = idx_o
    prob_ref[...] = prob_o
    pre_ref[...] = pre_o
    run_ref[...] = run_ref[...] + jnp.sum(cnt, axis=0, keepdims=True)
    cnt_ref[...] = run_ref[...]


def _router(x, w, b, tm=256):
    t, dm = x.shape
    tm = min(tm, t)
    return pl.pallas_call(
        _router_kernel,
        grid=(t // tm,),
        in_specs=[pl.BlockSpec((tm, dm), lambda i: (i, 0)),
                  pl.BlockSpec((dm, N_EXPERTS), lambda i: (0, 0)),
                  pl.BlockSpec((1, N_EXPERTS), lambda i: (0, 0))],
        out_specs=[pl.BlockSpec((tm, LANES), lambda i: (i, 0)),
                   pl.BlockSpec((tm, LANES), lambda i: (i, 0)),
                   pl.BlockSpec((tm, LANES), lambda i: (i, 0)),
                   pl.BlockSpec((1, N_EXPERTS), lambda i: (0, 0))],
        out_shape=[jax.ShapeDtypeStruct((t, LANES), I32),
                   jax.ShapeDtypeStruct((t, LANES), F32),
                   jax.ShapeDtypeStruct((t, LANES), I32),
                   jax.ShapeDtypeStruct((1, N_EXPERTS), F32)],
        scratch_shapes=[pltpu.VMEM((1, N_EXPERTS), F32)],
        compiler_params=_cparams(("arbitrary",)),
        name="moe_router",
    )(x, w, b.reshape(1, N_EXPERTS))


def _dispatch_kernel(gstart_ref, x_ref, idx_ref, pre_ref, xs_in_ref, xs_ref, sem):
    del xs_in_ref
    n = idx_ref.shape[0]

    def row_copy(j):
        dest = gstart_ref[idx_ref[j]] + pre_ref[j]
        return pltpu.make_async_copy(x_ref.at[pl.ds(j // TOP_K, 1)], xs_ref.at[pl.ds(dest, 1)], sem)

    def issue(j, carry):
        row_copy(j).start()
        return carry

    def drain(j, carry):
        row_copy(j).wait()
        return carry

    lax.fori_loop(0, n, issue, 0)
    lax.fori_loop(0, n, drain, 0)


def _dispatch(x, idx_flat, pre_flat, gstart, nrows):
    t, dm = x.shape
    tb = min(MOE_TB, t)
    xs0 = jnp.zeros((nrows, dm), x.dtype)
    grid_spec = pltpu.PrefetchScalarGridSpec(
        num_scalar_prefetch=1,
        grid=(t // tb,),
        in_specs=[pl.BlockSpec((tb, dm), lambda i, gs: (i, 0)),
                  pl.BlockSpec((tb * TOP_K,), lambda i, gs: (i,), memory_space=pltpu.SMEM),
                  pl.BlockSpec((tb * TOP_K,), lambda i, gs: (i,), memory_space=pltpu.SMEM),
                  pl.BlockSpec(memory_space=pl.ANY)],
        out_specs=pl.BlockSpec(memory_space=pl.ANY),
        scratch_shapes=[pltpu.SemaphoreType.DMA(())])
    return pl.pallas_call(
        _dispatch_kernel,
        grid_spec=grid_spec,
        out_shape=jax.ShapeDtypeStruct((nrows, dm), x.dtype),
        input_output_aliases={4: 0},
        compiler_params=_cparams(("arbitrary",)),
        name="moe_dispatch",
    )(gstart, x, idx_flat, pre_flat, xs0)


def _expert_kernel(bexp_ref, nvalid_ref, x_ref, wg_ref, wu_ref, wd_ref, bg_ref, bu_ref, bd_ref, y_ref,
                   wg_s, wu_s, wd_s):
    i = pl.program_id(0)
    valid = i < nvalid_ref[0]
    changed = jnp.logical_or(i == 0, bexp_ref[i] != bexp_ref[jnp.maximum(i - 1, 0)])

    @pl.when(jnp.logical_and(valid, changed))
    def _():
        wg_s[...] = wg_ref[...].astype(BF16)
        wu_s[...] = wu_ref[...].astype(BF16)
        wd_s[...] = wd_ref[...].astype(BF16)

    @pl.when(valid)
    def _():
        x = x_ref[...].astype(BF16)
        glin = jnp.minimum(_dot(x, wg_s[...]) + bg_ref[...], SWIGLU_LIMIT)
        ulin = jnp.clip(_dot(x, wu_s[...]) + bu_ref[...], -SWIGLU_LIMIT, SWIGLU_LIMIT)
        act = (ulin + 1.0) * glin * _sigmoid(SWIGLU_ALPHA * glin)
        y_ref[...] = _dot(act.astype(BF16), wd_s[...]) + bd_ref[...]

    @pl.when(jnp.logical_not(valid))
    def _():
        y_ref[...] = jnp.zeros_like(y_ref)


def _experts(xs, bexp, nvalid, w_gate, b_gate, w_up, b_up, w_down, b_down):
    nrows, dm = xs.shape
    tm = MOE_TM
    nb = nrows // tm
    ff = w_gate.shape[-1]

    def row_map(i, be, nv):
        return (jnp.minimum(i, nv[0] - 1), 0)

    def w_map(i, be, nv):
        return (be[i], 0, 0)

    grid_spec = pltpu.PrefetchScalarGridSpec(
        num_scalar_prefetch=2,
        grid=(nb,),
        in_specs=[pl.BlockSpec((tm, dm), row_map),
                  pl.BlockSpec((None, dm, ff), w_map),
                  pl.BlockSpec((None, dm, ff), w_map),
                  pl.BlockSpec((None, ff, dm), w_map),
                  pl.BlockSpec((None, 1, ff), w_map),
                  pl.BlockSpec((None, 1, ff), w_map),
                  pl.BlockSpec((None, 1, dm), w_map)],
        out_specs=pl.BlockSpec((tm, dm), lambda i, be, nv: (i, 0)),
        scratch_shapes=[pltpu.VMEM((dm, ff), BF16), pltpu.VMEM((dm, ff), BF16),
                        pltpu.VMEM((ff, dm), BF16)])
    return pl.pallas_call(
        _expert_kernel,
        grid_spec=grid_spec,
        out_shape=jax.ShapeDtypeStruct((nrows, dm), F32),
        compiler_params=_cparams(("arbitrary",)),
        name="moe_experts",
    )(bexp, nvalid, xs, w_gate, w_up, w_down, b_gate.reshape(N_EXPERTS, 1, ff),
      b_up.reshape(N_EXPERTS, 1, ff), b_down.reshape(N_EXPERTS, 1, dm))


def _combine_kernel(gstart_ref, h_ref, prob_ref, idx_ref, pre_ref, ys_ref, g_ref, b_ref, o_ref, obf_ref,
                    ybuf, sem):
    n = idx_ref.shape[0]

    def row_copy(j):
        src = gstart_ref[idx_ref[j]] + pre_ref[j]
        return pltpu.make_async_copy(ys_ref.at[pl.ds(src, 1)],
                                     ybuf.at[j % TOP_K, pl.ds(j // TOP_K, 1)], sem)

    def issue(j, carry):
        row_copy(j).start()
        return carry

    def drain(j, carry):
        row_copy(j).wait()
        return carry

    lax.fori_loop(0, n, issue, 0)
    lax.fori_loop(0, n, drain, 0)
    prob = prob_ref[...]
    ff = prob[:, 0:1] * ybuf[0]
    for kk in range(1, TOP_K):
        ff = ff + prob[:, kk:kk + 1] * ybuf[kk]
    y = _layer_norm_rows(DEEPNORM_ALPHA * h_ref[...] + ff, g_ref[...], b_ref[...])
    o_ref[...] = y
    obf_ref[...] = y.astype(BF16)


def _combine(h, prob, idx_flat, pre_flat, gstart, ys, g, b):
    t, dm = h.shape
    tb = min(MOE_TB, t)
    grid_spec = pltpu.PrefetchScalarGridSpec(
        num_scalar_prefetch=1,
        grid=(t // tb,),
        in_specs=[pl.BlockSpec((tb, dm), lambda i, gs: (i, 0)),
                  pl.BlockSpec((tb, LANES), lambda i, gs: (i, 0)),
                  pl.BlockSpec((tb * TOP_K,), lambda i, gs: (i,), memory_space=pltpu.SMEM),
                  pl.BlockSpec((tb * TOP_K,), lambda i, gs: (i,), memory_space=pltpu.SMEM),
                  pl.BlockSpec(memory_space=pl.ANY),
                  pl.BlockSpec((1, dm), lambda i, gs: (0, 0)),
                  pl.BlockSpec((1, dm), lambda i, gs: (0, 0))],
        out_specs=[pl.BlockSpec((tb, dm), lambda i, gs: (i, 0)),
                   pl.BlockSpec((tb, dm), lambda i, gs: (i, 0))],
        scratch_shapes=[pltpu.VMEM((TOP_K, tb, dm), F32), pltpu.SemaphoreType.DMA(())])
    return pl.pallas_call(
        _combine_kernel,
        grid_spec=grid_spec,
        out_shape=[jax.ShapeDtypeStruct((t, dm), F32), jax.ShapeDtypeStruct((t, dm), BF16)],
        compiler_params=_cparams(("arbitrary",)),
        name="moe_combine",
    )(gstart, h, prob, idx_flat, pre_flat, ys, g.reshape(1, dm), b.reshape(1, dm))


def _moe_layer(h, w_router, b_router, w_gate, b_gate, w_up, b_up, w_down, b_down, ln_g, ln_b):
    t, dm = h.shape
    tm = MOE_TM
    idx, prob, pre, counts = _router(h, w_router, b_router)
    counts = counts.reshape(N_EXPERTS).astype(I32)
    nblk = (counts + tm - 1) // tm
    blk_end = jnp.cumsum(nblk)
    gstart = ((blk_end - nblk) * tm).astype(I32)
    nvalid = blk_end[-1:].astype(I32)
    nb = (t * TOP_K) // tm + N_EXPERTS
    blk = jnp.minimum(jnp.arange(nb, dtype=I32), nvalid[0] - 1)
    bexp = jnp.sum((blk[:, None] >= blk_end[None, :]).astype(I32), axis=1).astype(I32)
    idx_flat = idx[:, :TOP_K].reshape(t * TOP_K)
    pre_flat = pre[:, :TOP_K].reshape(t * TOP_K)
    xs = _dispatch(h, idx_flat, pre_flat, gstart, nb * tm)
    ys = _experts(xs, bexp, nvalid, w_gate, b_gate, w_up, b_up, w_down, b_down)
    return _combine(h, prob, idx_flat, pre_flat, gstart, ys, ln_g, ln_b)


def _even_layer(h, hb, w_in, w_out, s5_b_re, s5_b_im, s5_c_re, s5_c_im, s5_d, s5_a_re, s5_a_im,
                s5_log_dt, s5_w_glu, s5_b_glu, ln_g, ln_b):
    t = h.shape[0]
    p = _matmul(hb, w_in.astype(BF16), F32, 1024, 512)
    o_ret = _retention(p, t)
    u = p[:, 2 * RET_HEADS * RET_DK + 2 * RET_WIDTH:]
    y = _s5(u, s5_b_re, s5_b_im, s5_c_re, s5_c_im, s5_d, s5_a_re, s5_a_im, s5_log_dt)
    y = _matmul_glu(y, s5_w_glu.astype(BF16), s5_b_glu)
    mixed = jnp.concatenate([o_ret, y], axis=-1)
    return _matmul_ln(mixed, w_out.astype(BF16), h, ln_g, ln_b)


def _odd_layer(h, hb, w_in, w_out, conv_w, a_log, dt_bias, norm_w, ln_g, ln_b):
    t = h.shape[0]
    nmain = 2 * GDN_HEADS * GDN_DK + 2 * GDN_HEADS * GDN_DV
    p = _matmul(hb, w_in[:, :nmain].astype(BF16), F32, 1024, 512)
    ab = _matmul(hb, w_in[:, nmain:].astype(BF16), F32, 1024, 2 * GDN_HEADS)
    o = _gated_deltanet(p, ab, conv_w, a_log, dt_bias, norm_w, t)
    return _matmul_ln(o, w_out.astype(BF16), h, ln_g, ln_b)


def kernel(x, ev_w_in, ev_w_out, ev_s5_b_re, ev_s5_b_im, ev_s5_c_re, ev_s5_c_im, ev_s5_d, ev_s5_a_re,
           ev_s5_a_im, ev_s5_log_dt, ev_s5_w_glu, ev_s5_b_glu, ev_ln_g, ev_ln_b, od_w_in, od_w_out,
           od_conv_w, od_a_log, od_dt_bias, od_norm_w, od_ln_g, od_ln_b, moe_w_router, moe_b_router,
           moe_w_gate, moe_b_gate, moe_w_up, moe_b_up, moe_w_down, moe_b_down, moe_ln_g, moe_ln_b):
    bsz, seq, dm = x.shape
    outs = []
    for bi in range(bsz):
        h = x[bi]
        hb = h.astype(BF16)
        for layer in range(DEPTH):
            p = layer // 2
            if layer % 2 == 0:
                h, hb = _even_layer(h, hb, ev_w_in[p], ev_w_out[p], ev_s5_b_re[p], ev_s5_b_im[p],
                                    ev_s5_c_re[p], ev_s5_c_im[p], ev_s5_d[p], ev_s5_a_re[p],
                                    ev_s5_a_im[p], ev_s5_log_dt[p], ev_s5_w_glu[p], ev_s5_b_glu[p],
                                    ev_ln_g[p], ev_ln_b[p])
            else:
                h, hb = _odd_layer(h, hb, od_w_in[p], od_w_out[p], od_conv_w[p], od_a_log[p],
                                   od_dt_bias[p], od_norm_w[p], od_ln_g[p], od_ln_b[p])
            h, hb = _moe_layer(h, moe_w_router[layer], moe_b_router[layer], moe_w_gate[layer],
                               moe_b_gate[layer], moe_w_up[layer], moe_b_up[layer], moe_w_down[layer],
                               moe_b_down[layer], moe_ln_g[layer], moe_ln_b[layer])
        outs.append(h)
    return jnp.stack(outs, axis=0)
```

```python
import functools
import math

import jax
import jax.numpy as jnp
from jax import lax
from jax.experimental import pallas as pl
from jax.experimental.pallas import tpu as pltpu

F32 = jnp.float32
BF16 = jnp.bfloat16
I32 = jnp.int32
HIGHEST = lax.Precision.HIGHEST

D_MODEL = 4096
DEPTH = 4
CHUNK = 64
RET_HEADS = 8
RET_WIDTH = D_MODEL // 2
RET_DV = RET_WIDTH // RET_HEADS
RET_DK = RET_DV // 2
ROPE_BASE = 10000.0
S5_WIDTH = D_MODEL - RET_WIDTH
S5_GROUP = 16
S5_GROUPS = S5_WIDTH // S5_GROUP
S5_STATE = 64
EVEN_QKVG = 2 * RET_HEADS * RET_DK + 2 * RET_WIDTH
GDN_HEADS = 16
GDN_DV = D_MODEL // GDN_HEADS
GDN_DK = GDN_DV // 2
CONV_K = 4
GDN_MAIN = 2 * GDN_HEADS * GDN_DK + 2 * GDN_HEADS * GDN_DV
N_EXPERTS = 32
TOP_K = 4
EXPERT_FF = D_MODEL // 16
SWIGLU_LIMIT = 7.0
SWIGLU_ALPHA = 1.702
DEEPNORM_ALPHA = (2 * DEPTH) ** 0.25
LN_EPS = 1e-5
NORM_EPS = 1e-6

V7X_VMEM_LIMIT_BYTES = 56 * 1024 * 1024
LANES = 128
BF16_SUBLANES = 16
S5_SUB = 16
RET_BLOCK = 256
GDN_BLOCK = 256
GDN_PREP_HEADS = 2
MOE_TM = 256
MOE_TB = 128
DMA_UNROLL = 8
NEG_BIG = -3.0e38


def _cparams(sem, vmem=V7X_VMEM_LIMIT_BYTES):
    return pltpu.CompilerParams(dimension_semantics=sem, vmem_limit_bytes=vmem)


def _dot(a, b):
    return jnp.dot(a, b, preferred_element_type=F32)


def _dot_nt(a, b):
    return lax.dot_general(a, b, (((1,), (1,)), ((), ())), preferred_element_type=F32)


def _dot_tn(a, b):
    return lax.dot_general(a, b, (((0,), (0,)), ((), ())), preferred_element_type=F32)


def _dot_hi(a, b):
    return jnp.dot(a, b, preferred_element_type=F32, precision=HIGHEST)


def _sigmoid(x):
    return 1.0 / (1.0 + jnp.exp(-x))


def _mm_kernel(x_ref, w_ref, o_ref):
    o_ref[...] = _dot(x_ref[...].astype(BF16), w_ref[...]).astype(o_ref.dtype)


def _matmul(x, w, n, out_dtype, tm, tn):
    m, k = x.shape
    tm, tn = min(tm, m), min(tn, n)
    return pl.pallas_call(
        _mm_kernel,
        grid=(m // tm, n // tn),
        in_specs=[pl.BlockSpec((tm, k), lambda i, j: (i, 0)),
                  pl.BlockSpec((k, tn), lambda i, j: (0, j))],
        out_specs=pl.BlockSpec((tm, tn), lambda i, j: (i, j)),
        out_shape=jax.ShapeDtypeStruct((m, n), out_dtype),
        compiler_params=_cparams(("parallel", "parallel")),
        name="proj_matmul",
    )(x, w)


def _layer_norm_rows(z, g, b):
    mu = jnp.mean(z, axis=-1, keepdims=True)
    zc = z - mu
    var = jnp.mean(zc * zc, axis=-1, keepdims=True)
    return zc * lax.rsqrt(var + LN_EPS) * g + b


def _mm_ln_kernel(x_ref, w_ref, res_ref, g_ref, b_ref, o_ref, obf_ref, *, nk):
    k = pl.program_id(1)

    @pl.when(k == 0)
    def _():
        o_ref[...] = jnp.zeros_like(o_ref)

    o_ref[...] += _dot(x_ref[...].astype(BF16), w_ref[...])

    @pl.when(k == nk - 1)
    def _():
        y = _layer_norm_rows(DEEPNORM_ALPHA * res_ref[...] + o_ref[...], g_ref[...], b_ref[...])
        o_ref[...] = y
        obf_ref[...] = y.astype(BF16)


def _matmul_ln(x, w, res, g, b, tm=512, tk=256):
    m, kdim = x.shape
    n = w.shape[1]
    tm, tk = min(tm, m), min(tk, kdim)
    nk = kdim // tk
    return pl.pallas_call(
        functools.partial(_mm_ln_kernel, nk=nk),
        grid=(m // tm, nk),
        in_specs=[pl.BlockSpec((tm, tk), lambda i, k: (i, k)),
                  pl.BlockSpec((tk, n), lambda i, k: (k, 0)),
                  pl.BlockSpec((tm, n), lambda i, k: (i, 0)),
                  pl.BlockSpec((1, n), lambda i, k: (0, 0)),
                  pl.BlockSpec((1, n), lambda i, k: (0, 0))],
        out_specs=[pl.BlockSpec((tm, n), lambda i, k: (i, 0)),
                   pl.BlockSpec((tm, n), lambda i, k: (i, 0))],
        out_shape=[jax.ShapeDtypeStruct((m, n), F32), jax.ShapeDtypeStruct((m, n), BF16)],
        compiler_params=_cparams(("parallel", "arbitrary")),
        name="proj_layernorm",
    )(x, w, res, g.reshape(1, n), b.reshape(1, n))


def _mm_glu_kernel(x_ref, yt_ref, w_ref, b_ref, o_ref):
    z = _dot(x_ref[...], w_ref[...]) + b_ref[...]
    o_ref[...] = (yt_ref[...].astype(F32) * _sigmoid(z)).astype(o_ref.dtype)


def _matmul_glu(y, w, b, tm=1024, tn=512):
    m, k = y.shape
    n = w.shape[1]
    tm, tn = min(tm, m), min(tn, n)
    return pl.pallas_call(
        _mm_glu_kernel,
        grid=(m // tm, n // tn),
        in_specs=[pl.BlockSpec((tm, k), lambda i, j: (i, 0)),
                  pl.BlockSpec((tm, tn), lambda i, j: (i, j)),
                  pl.BlockSpec((k, tn), lambda i, j: (0, j)),
                  pl.BlockSpec((1, tn), lambda i, j: (0, j))],
        out_specs=pl.BlockSpec((tm, tn), lambda i, j: (i, j)),
        out_shape=jax.ShapeDtypeStruct((m, n), BF16),
        compiler_params=_cparams(("parallel", "parallel")),
        name="s5_glu",
    )(y, y, w, b.reshape(1, n))


def _ret_kernel(bdec_ref, q_ref, k_ref, v_ref, gate_ref, cos_ref, sin_ref, dmask_ref, qdec_ref,
                kdec_ref, o_ref, s_ref):
    h = pl.program_id(0)

    @pl.when(pl.program_id(1) == 0)
    def _():
        s_ref[...] = jnp.zeros_like(s_ref)

    cos = cos_ref[...]
    sin = sin_ref[...]
    q = q_ref[...].astype(F32)
    k = k_ref[...].astype(F32)
    half = RET_DK // 2
    qr = q * cos + pltpu.roll(q, half, 1) * sin
    kr = (k * cos + pltpu.roll(k, half, 1) * sin) * (RET_DK ** -0.5)
    v = v_ref[...].astype(BF16)
    scores = _dot_nt(qr.astype(BF16), kr.astype(BF16)) * dmask_ref[0]
    o = _dot(scores.astype(BF16), v)
    state = s_ref[...]
    o = o + _dot((qr * qdec_ref[0]).astype(BF16), state.astype(BF16))
    s_ref[...] = state * bdec_ref[h] + _dot_tn((kr * kdec_ref[0]).astype(BF16), v)
    mu = jnp.mean(o, axis=-1, keepdims=True)
    oc = o - mu
    var = jnp.mean(oc * oc, axis=-1, keepdims=True)
    on = oc * lax.rsqrt(var + NORM_EPS)
    gate = gate_ref[...].astype(F32)
    o_ref[...] = (on * (gate * _sigmoid(gate))).astype(o_ref.dtype)


def _retention_tables(t, bt):
    half = RET_DK // 2
    inv_freq = jnp.power(ROPE_BASE, -jnp.arange(half, dtype=F32) / half)
    ang = jnp.arange(t, dtype=F32)[:, None] * inv_freq[None, :]
    cos, sin = jnp.cos(ang), jnp.sin(ang)
    cos2 = jnp.concatenate([cos, cos], axis=-1)
    sin2 = jnp.concatenate([-sin, sin], axis=-1)
    log_gamma = jnp.log1p(-jnp.exp2(-5.0 - jnp.arange(RET_HEADS, dtype=F32)))
    pos = jnp.arange(bt, dtype=F32)
    dist = jnp.abs(pos[:, None] - pos[None, :])
    chunk_id = jnp.arange(bt) // CHUNK
    visible = chunk_id[None, :] <= chunk_id[:, None]
    dmask = jnp.where(visible[None], jnp.exp(log_gamma[:, None, None] * dist[None]), 0.0)
    qdec = jnp.exp(log_gamma[:, None] * (pos + 1.0))
    kdec = jnp.exp(log_gamma[:, None] * (bt - 1.0 - pos))
    qdec = jnp.broadcast_to(qdec[:, :, None], (RET_HEADS, bt, RET_DK))
    kdec = jnp.broadcast_to(kdec[:, :, None], (RET_HEADS, bt, RET_DK))
    bdec = jnp.exp(log_gamma * bt)
    return cos2, sin2, dmask, qdec, kdec, bdec


def _retention(p, t):
    bt = min(RET_BLOCK, t)
    cos2, sin2, dmask, qdec, kdec, bdec = _retention_tables(t, bt)
    kq = RET_HEADS
    kv = 2 * RET_HEADS * RET_DK // RET_DV
    kg = kv + RET_HEADS
    grid_spec = pltpu.PrefetchScalarGridSpec(
        num_scalar_prefetch=0,
        grid=(RET_HEADS, t // bt),
        in_specs=[pl.BlockSpec(memory_space=pltpu.SMEM),
                  pl.BlockSpec((bt, RET_DK), lambda h, i: (i, h)),
                  pl.BlockSpec((bt, RET_DK), lambda h, i: (i, kq + h)),
                  pl.BlockSpec((bt, RET_DV), lambda h, i: (i, kv + h)),
                  pl.BlockSpec((bt, RET_DV), lambda h, i: (i, kg + h)),
                  pl.BlockSpec((bt, RET_DK), lambda h, i: (i, 0)),
                  pl.BlockSpec((bt, RET_DK), lambda h, i: (i, 0)),
                  pl.BlockSpec((1, bt, bt), lambda h, i: (h, 0, 0)),
                  pl.BlockSpec((1, bt, RET_DK), lambda h, i: (h, 0, 0)),
                  pl.BlockSpec((1, bt, RET_DK), lambda h, i: (h, 0, 0))],
        out_specs=pl.BlockSpec((bt, RET_DV), lambda h, i: (i, h)),
        scratch_shapes=[pltpu.VMEM((RET_DK, RET_DV), F32)])
    return pl.pallas_call(
        _ret_kernel,
        grid_spec=grid_spec,
        out_shape=jax.ShapeDtypeStruct((t, RET_WIDTH), BF16),
        compiler_params=_cparams(("parallel", "arbitrary")),
        name="retention",
    )(bdec, p, p, p, p, cos2, sin2, dmask, qdec, kdec)


def _gelu_tanh(x):
    c = math.sqrt(2.0 / math.pi)
    return 0.5 * x * (1.0 + jnp.tanh(c * (x + 0.044715 * (x * x * x))))


def _s5_kernel(u_ref, m_ref, bm_ref, cm_ref, pw_ref, d_ref, y_ref, *, nlev):
    u = u_ref[0].astype(F32)
    ub = u.astype(BF16)
    nc = u.shape[0]
    hstate = _dot(ub, bm_ref[0])
    row = lax.broadcasted_iota(I32, hstate.shape, 0)
    pw = pw_ref[0]
    for lev in range(nlev):
        s = 1 << lev
        hs = jnp.where(row >= s, pltpu.roll(hstate, s, 0), 0.0)
        hstate = (hstate + pw[2 * lev:2 * lev + 1, :] * hs
                  + pw[2 * lev + 1:2 * lev + 2, :] * pltpu.roll(hs, S5_STATE, 1))
    if nc > 1:
        hprev = jnp.where(row >= 1, pltpu.roll(hstate, 1, 0), 0.0)
    else:
        hprev = jnp.zeros_like(hstate)
    y = _dot(ub, m_ref[0]) + _dot(hprev.astype(BF16), cm_ref[0]) + d_ref[0] * u
    y_ref[0] = _gelu_tanh(y).astype(y_ref.dtype)


def _s5_tables(b_re, b_im, c_re, c_im, d, a_re, a_im, log_dt, nc):
    g, p, i = S5_GROUPS, S5_STATE, S5_GROUP
    l = S5_SUB
    dt = jnp.exp(log_dt)[:, None]
    lam_re = jnp.minimum(a_re, -1e-4)
    lam_im = a_im
    mag = jnp.exp(lam_re * dt)
    ab_re = mag * jnp.cos(lam_im * dt)
    ab_im = mag * jnp.sin(lam_im * dt)
    den = jnp.square(lam_re) + jnp.square(lam_im)
    f_re = ((ab_re - 1.0) * lam_re + ab_im * lam_im) / den
    f_im = (ab_im * lam_re - (ab_re - 1.0) * lam_im) / den
    bb_re = f_re[..., None] * b_re - f_im[..., None] * b_im
    bb_im = f_re[..., None] * b_im + f_im[..., None] * b_re

    def apow(n):
        n = jnp.asarray(n, F32)[..., None, None]
        m = jnp.exp(n * (lam_re * dt))
        return m * jnp.cos(n * (lam_im * dt)), m * jnp.sin(n * (lam_im * dt))

    wr, wi = apow(jnp.arange(l))
    xr = wr[..., None] * bb_re[None] - wi[..., None] * bb_im[None]
    xi = wr[..., None] * bb_im[None] + wi[..., None] * bb_re[None]
    kern = (jnp.einsum('gip,jgpk->gjik', c_re, xr, precision=HIGHEST)
            - jnp.einsum('gip,jgpk->gjik', c_im, xi, precision=HIGHEST))
    lag = jnp.arange(l)[None, :] - jnp.arange(l)[:, None]
    toe = kern[:, jnp.clip(lag, 0, l - 1)]
    toe = jnp.where((lag >= 0)[None, :, :, None, None], toe, 0.0)
    m_mat = toe.transpose(0, 1, 4, 2, 3).reshape(g, l * i, l * i)
    wr, wi = apow(l - 1 - jnp.arange(l))
    br = wr[..., None] * bb_re[None] - wi[..., None] * bb_im[None]
    bi = wr[..., None] * bb_im[None] + wi[..., None] * bb_re[None]
    bm = jnp.concatenate([br, bi], axis=2)
    bm = bm.transpose(1, 0, 3, 2).reshape(g, l * i, 2 * p)
    wr, wi = apow(jnp.arange(l) + 1.0)
    cr = (c_re.transpose(0, 2, 1)[None] * wr[..., None]
          - c_im.transpose(0, 2, 1)[None] * wi[..., None])
    ci = (-c_re.transpose(0, 2, 1)[None] * wi[..., None]
          - c_im.transpose(0, 2, 1)[None] * wr[..., None])
    cm = jnp.concatenate([cr, ci], axis=2)
    cm = cm.transpose(1, 2, 0, 3).reshape(g, 2 * p, l * i)
    nlev = max(1, int(math.ceil(math.log2(nc)))) if nc > 1 else 0
    if nlev:
        wr, wi = apow(float(l) * (2.0 ** jnp.arange(nlev)))
        pw = jnp.stack([jnp.concatenate([wr, wr], axis=-1),
                        jnp.concatenate([-wi, wi], axis=-1)], axis=1)
        pw = pw.reshape(2 * nlev, g, 2 * p).transpose(1, 0, 2)
    else:
        pw = jnp.zeros((g, 2, 2 * p), F32)
    dd = jnp.tile(d.reshape(g, 1, i), (1, l, 1)).reshape(g, 1, l * i)
    return m_mat.astype(BF16), bm.astype(BF16), cm.astype(BF16), pw, dd, nlev


def _s5_channel_major(w, axis):
    shape = w.shape
    w = w.reshape(shape[:axis] + (S5_GROUPS, S5_GROUP) + shape[axis + 1:])
    return jnp.swapaxes(w, axis, axis + 1).reshape(shape)


def _s5(u, b_re, b_im, c_re, c_im, d, a_re, a_im, log_dt):
    t = u.shape[0]
    g, l, i = S5_GROUPS, S5_SUB, S5_GROUP
    nc = t // l
    m_mat, bm, cm, pw, dd, nlev = _s5_tables(b_re, b_im, c_re, c_im, d, a_re, a_im, log_dt, nc)
    ug = u.reshape(t * i, g).T.reshape(g, nc, l * i)
    npw = pw.shape[1]
    y = pl.pallas_call(
        functools.partial(_s5_kernel, nlev=nlev),
        grid=(g,),
        in_specs=[pl.BlockSpec((1, nc, l * i), lambda j: (j, 0, 0)),
                  pl.BlockSpec((1, l * i, l * i), lambda j: (j, 0, 0)),
                  pl.BlockSpec((1, l * i, 2 * S5_STATE), lambda j: (j, 0, 0)),
                  pl.BlockSpec((1, 2 * S5_STATE, l * i), lambda j: (j, 0, 0)),
                  pl.BlockSpec((1, npw, 2 * S5_STATE), lambda j: (j, 0, 0)),
                  pl.BlockSpec((1, 1, l * i), lambda j: (j, 0, 0))],
        out_specs=pl.BlockSpec((1, nc, l * i), lambda j: (j, 0, 0)),
        out_shape=jax.ShapeDtypeStruct((g, nc, l * i), BF16),
        compiler_params=_cparams(("parallel",)),
        name="s5_scan",
    )(ug, m_mat, bm, cm, pw, dd)
    return y.reshape(g, t * i).T.reshape(t, g * i)


def _causal_conv_silu(x, halo, w):
    row = lax.broadcasted_iota(I32, x.shape, 0)
    y = w[CONV_K - 1:CONV_K, :] * x
    for j in range(1, CONV_K):
        head = jnp.concatenate([pltpu.roll(halo, j, 0)] * (x.shape[0] // halo.shape[0]), axis=0)
        shifted = jnp.where(row >= j, pltpu.roll(x, j, 0), head)
        y = y + w[CONV_K - 1 - j:CONV_K - j, :] * shifted
    return y * _sigmoid(y)


def _softplus(x):
    return jnp.maximum(x, 0.0) + jnp.log(1.0 + jnp.exp(-jnp.abs(x)))


def _gdn_prep_kernel(nega_ref, dtb_ref, q_ref, k_ref, v_ref, qh_ref, kh_ref, vh_ref, wq_ref, wk_ref,
                     wv_ref, ab_ref, abt_ref, wvo_ref, kco_ref, qgo_ref, kdo_ref, ato_ref, ego_ref):
    hp = pl.program_id(0)
    keep = (pl.program_id(1) > 0).astype(F32)
    q_all = _causal_conv_silu(q_ref[...].astype(F32), qh_ref[...].astype(F32) * keep, wq_ref[...])
    k_all = _causal_conv_silu(k_ref[...].astype(F32), kh_ref[...].astype(F32) * keep, wk_ref[...])
    v_all = _causal_conv_silu(v_ref[...].astype(F32), vh_ref[...].astype(F32) * keep, wv_ref[...])
    bt = q_all.shape[0]
    ab = ab_ref[...]
    lane = lax.broadcasted_iota(I32, ab.shape, 1)
    abt = abt_ref[...]
    sub = lax.broadcasted_iota(I32, abt.shape, 0)

    bi = lax.broadcasted_iota(I32, (bt, bt), 0)
    bj = lax.broadcasted_iota(I32, (bt, bt), 1)
    same = (bi // CHUNK) == (bj // CHUNK)
    lower = jnp.logical_and(same, bi >= bj)
    strict = jnp.logical_and(same, bi > bj)
    upper = jnp.logical_and(same, bi <= bj)
    eye = (bi == bj).astype(F32)

    for hh in range(GDN_PREP_HEADS):
        h = hp * GDN_PREP_HEADS + hh
        q = q_all[:, hh * GDN_DK:(hh + 1) * GDN_DK]
        k = k_all[:, hh * GDN_DK:(hh + 1) * GDN_DK]
        v = v_all[:, hh * GDN_DV:(hh + 1) * GDN_DV]
        q = q * lax.rsqrt(jnp.sum(q * q, axis=-1, keepdims=True) + NORM_EPS) * (GDN_DK ** -0.5)
        k = k * lax.rsqrt(jnp.sum(k * k, axis=-1, keepdims=True) + NORM_EPS)
        neg_a = nega_ref[h]
        dtb = dtb_ref[h]
        a_col = jnp.sum(jnp.where(lane == h, ab, 0.0), axis=1, keepdims=True)
        b_col = jnp.sum(jnp.where(lane == GDN_HEADS + h, ab, 0.0), axis=1, keepdims=True)
        a_row = jnp.sum(jnp.where(sub == h, abt, 0.0), axis=0, keepdims=True)
        g_col = neg_a * _softplus(a_col + dtb)
        g_row = neg_a * _softplus(a_row + dtb)
        beta = _sigmoid(b_col)
        gcum_col = jnp.sum(jnp.where(lower, g_row, 0.0), axis=1, keepdims=True)
        gcum_row = jnp.sum(jnp.where(upper, g_col, 0.0), axis=0, keepdims=True)
        gtot_col = jnp.sum(jnp.where(same, g_row, 0.0), axis=1, keepdims=True)
        decay = jnp.where(lower, jnp.exp(jnp.where(lower, gcum_col - gcum_row, 0.0)), 0.0)
        k_beta = k * beta
        v_beta = v * beta
        k16 = k.astype(BF16)
        lmat = jnp.where(strict, _dot_nt(k_beta.astype(BF16), k16) * decay, 0.0)
        tinv = eye - lmat
        l16 = lmat.astype(BF16)
        pw = _dot(l16, l16)
        n = 2
        while True:
            tinv = tinv + _dot(tinv.astype(BF16), pw.astype(BF16))
            n *= 2
            if n >= CHUNK:
                break
            p16 = pw.astype(BF16)
            pw = _dot(p16, p16)
        t16 = tinv.astype(BF16)
        eg = jnp.exp(gcum_col)
        w_v = _dot(t16, v_beta.astype(BF16))
        k_cum = _dot(t16, (k_beta * eg).astype(BF16))
        attn = jnp.where(lower, _dot_nt(q.astype(BF16), k16) * decay, 0.0)
        folded = attn[:, :LANES]
        for c in range(1, bt // LANES):
            folded = folded + attn[:, c * LANES:(c + 1) * LANES]
        folded = folded + pltpu.roll(folded, CHUNK, 1)
        wvo_ref[:, hh * GDN_DV:(hh + 1) * GDN_DV] = w_v.astype(BF16)
        kco_ref[:, hh * GDN_DK:(hh + 1) * GDN_DK] = k_cum.astype(BF16)
        qgo_ref[:, hh * GDN_DK:(hh + 1) * GDN_DK] = (q * eg).astype(BF16)
        kdo_ref[:, hh * GDN_DK:(hh + 1) * GDN_DK] = (k * jnp.exp(gtot_col - gcum_col)).astype(BF16)
        ato_ref[:, hh * LANES:(hh + 1) * LANES] = folded.astype(BF16)
        ego_ref[:, hh * LANES:(hh + 1) * LANES] = jnp.broadcast_to(jnp.exp(gtot_col), (bt, LANES))


def _gdn_scan_kernel(wv_ref, kc_ref, qg_ref, kd_ref, at_ref, eg_ref, gate_ref, nw_ref, o_ref, s_ref):
    @pl.when(pl.program_id(0) == 0)
    def _():
        s_ref[...] = jnp.zeros_like(s_ref)

    nw = nw_ref[...]

    def chunk_step(c, carry):
        r0 = pl.multiple_of(c * CHUNK, CHUNK)
        rows = pl.ds(r0, CHUNK)
        for h in range(GDN_HEADS):
            ck = slice(h * GDN_DK, (h + 1) * GDN_DK)
            cv = slice(h * GDN_DV, (h + 1) * GDN_DV)
            state = s_ref[h]
            s16 = state.astype(BF16)
            v_new = wv_ref[rows, cv].astype(F32) - _dot(kc_ref[rows, ck], s16)
            vn16 = v_new.astype(BF16)
            attn = at_ref[rows, h * LANES:h * LANES + CHUNK]
            o = _dot(qg_ref[rows, ck], s16) + _dot(attn, vn16)
            eg = eg_ref[pl.ds(r0, 1), h * LANES:h * LANES + 1]
            s_ref[h] = state * eg + _dot_tn(kd_ref[rows, ck], vn16)
            rms = lax.rsqrt(jnp.mean(o * o, axis=-1, keepdims=True) + NORM_EPS)
            gc = gate_ref[rows, cv].astype(F32)
            o_ref[rows, cv] = (o * rms * nw * (gc * _sigmoid(gc))).astype(o_ref.dtype)
        return carry

    lax.fori_loop(0, wv_ref.shape[0] // CHUNK, chunk_step, 0)


def _gated_deltanet(p, ab, conv_w, a_log, dt_bias, norm_w, t):
    bt = min(GDN_BLOCK, t)
    hq, hg = GDN_HEADS, GDN_PREP_HEADS
    wk_, wv_ = hg * GDN_DK, hg * GDN_DV
    kk = hq * GDN_DK // wk_
    kv = 2 * hq * GDN_DK // wv_
    hb = bt // BF16_SUBLANES

    def halo_map(off):
        return lambda g, i: (jnp.maximum(i * hb - 1, 0), off + g)

    prep_spec = pltpu.PrefetchScalarGridSpec(
        num_scalar_prefetch=0,
        grid=(hq // hg, t // bt),
        in_specs=[pl.BlockSpec(memory_space=pltpu.SMEM),
                  pl.BlockSpec(memory_space=pltpu.SMEM),
                  pl.BlockSpec((bt, wk_), lambda g, i: (i, g)),
                  pl.BlockSpec((bt, wk_), lambda g, i: (i, kk + g)),
                  pl.BlockSpec((bt, wv_), lambda g, i: (i, kv + g)),
                  pl.BlockSpec((BF16_SUBLANES, wk_), halo_map(0)),
                  pl.BlockSpec((BF16_SUBLANES, wk_), halo_map(kk)),
                  pl.BlockSpec((BF16_SUBLANES, wv_), halo_map(kv)),
                  pl.BlockSpec((CONV_K, wk_), lambda g, i: (0, g)),
                  pl.BlockSpec((CONV_K, wk_), lambda g, i: (0, kk + g)),
                  pl.BlockSpec((CONV_K, wv_), lambda g, i: (0, kv + g)),
                  pl.BlockSpec((bt, 2 * hq), lambda g, i: (i, 0)),
                  pl.BlockSpec((2 * hq, bt), lambda g, i: (0, i))],
        out_specs=[pl.BlockSpec((bt, wv_), lambda g, i: (i, g)),
                   pl.BlockSpec((bt, wk_), lambda g, i: (i, g)),
                   pl.BlockSpec((bt, wk_), lambda g, i: (i, g)),
                   pl.BlockSpec((bt, wk_), lambda g, i: (i, g)),
                   pl.BlockSpec((bt, hg * LANES), lambda g, i: (i, g)),
                   pl.BlockSpec((bt, hg * LANES), lambda g, i: (i, g))])
    w_v, k_cum, q_g, k_d, attn, eg = pl.pallas_call(
        _gdn_prep_kernel,
        grid_spec=prep_spec,
        out_shape=[jax.ShapeDtypeStruct((t, hq * GDN_DV), BF16),
                   jax.ShapeDtypeStruct((t, hq * GDN_DK), BF16),
                   jax.ShapeDtypeStruct((t, hq * GDN_DK), BF16),
                   jax.ShapeDtypeStruct((t, hq * GDN_DK), BF16),
                   jax.ShapeDtypeStruct((t, hq * LANES), BF16),
                   jax.ShapeDtypeStruct((t, hq * LANES), F32)],
        compiler_params=_cparams(("parallel", "parallel")),
        name="gdn_prep",
    )(-jnp.exp(a_log), dt_bias, p, p, p, p, p, p, conv_w, conv_w, conv_w, ab, ab.T)

    gate_blk = (2 * hq * GDN_DK + hq * GDN_DV) // (hq * GDN_DV)
    return pl.pallas_call(
        _gdn_scan_kernel,
        grid=(t // bt,),
        in_specs=[pl.BlockSpec((bt, hq * GDN_DV), lambda i: (i, 0)),
                  pl.BlockSpec((bt, hq * GDN_DK), lambda i: (i, 0)),
                  pl.BlockSpec((bt, hq * GDN_DK), lambda i: (i, 0)),
                  pl.BlockSpec((bt, hq * GDN_DK), lambda i: (i, 0)),
                  pl.BlockSpec((bt, hq * LANES), lambda i: (i, 0)),
                  pl.BlockSpec((bt, hq * LANES), lambda i: (i, 0)),
                  pl.BlockSpec((bt, hq * GDN_DV), lambda i: (i, gate_blk)),
                  pl.BlockSpec((1, GDN_DV), lambda i: (0, 0))],
        out_specs=pl.BlockSpec((bt, hq * GDN_DV), lambda i: (i, 0)),
        out_shape=jax.ShapeDtypeStruct((t, hq * GDN_DV), BF16),
        scratch_shapes=[pltpu.VMEM((hq, GDN_DK, GDN_DV), F32)],
        compiler_params=_cparams(("arbitrary",)),
        name="gdn_scan",
    )(w_v, k_cum, q_g, k_d, attn, eg, p, norm_w.reshape(1, GDN_DV))


def _router_kernel(x_ref, w_ref, b_ref, idx_ref, prob_ref, pre_ref, cnt_ref, run_ref):
    @pl.when(pl.program_id(0) == 0)
    def _():
        run_ref[...] = jnp.zeros_like(run_ref)

    logits = _dot_hi(x_ref[...], w_ref[...]) + b_ref[...]
    tm = logits.shape[0]
    lane = lax.broadcasted_iota(I32, logits.shape, 1)
    work = logits
    vals, sels, hots = [], [], []
    for _ in range(TOP_K):
        m = jnp.max(work, axis=1, keepdims=True)
        sel = jnp.min(jnp.where(work == m, lane, N_EXPERTS), axis=1, keepdims=True)
        hot = lane == sel
        vals.append(m)
        sels.append(sel)
        hots.append(hot)
        work = jnp.where(hot, NEG_BIG, work)
    exps = [jnp.exp(vk - vals[0]) for vk in vals]
    denom = exps[0] + exps[1] + exps[2] + exps[3]
    cnt = jnp.zeros(logits.shape, F32)
    for hot in hots:
        cnt = cnt + hot.astype(F32)
    ri = lax.broadcasted_iota(I32, (tm, tm), 0)
    rj = lax.broadcasted_iota(I32, (tm, tm), 1)
    tri = jnp.where(ri > rj, 1.0, 0.0).astype(BF16)
    pexcl = _dot(tri, cnt.astype(BF16)) + run_ref[...]
    lane_o = lax.broadcasted_iota(I32, (tm, LANES), 1)
    idx_o = jnp.zeros((tm, LANES), I32)
    prob_o = jnp.zeros((tm, LANES), F32)
    pre_o = jnp.zeros((tm, LANES), I32)
    for kk in range(TOP_K):
        pre_k = jnp.sum(jnp.where(hots[kk], pexcl, 0.0), axis=1, keepdims=True).astype(I32)
        idx_o = jnp.where(lane_o == kk, sels[kk], idx_o)
        prob_o = jnp.where(lane_o == kk, exps[kk] / denom, prob_o)
        pre_o = jnp.where(lane_o == kk, pre_k, pre_o)
    idx_ref[...] = idx_o
    prob_ref[...] = prob_o
    pre_ref[...] = pre_o
    run_ref[...] = run_ref[...] + jnp.sum(cnt, axis=0, keepdims=True)
    cnt_ref[...] = run_ref[...]


def _router(x, w, b, tm=256):
    t, dm = x.shape
    tm = min(tm, t)
    return pl.pallas_call(
        _router_kernel,
        grid=(t // tm,),
        in_specs=[pl.BlockSpec((tm, dm), lambda i: (i, 0)),
                  pl.BlockSpec((dm, N_EXPERTS), lambda i: (0, 0)),
                  pl.BlockSpec((1, N_EXPERTS), lambda i: (0, 0))],
        out_specs=[pl.BlockSpec((tm, LANES), lambda i: (i, 0)),
                   pl.BlockSpec((tm, LANES), lambda i: (i, 0)),
                   pl.BlockSpec((tm, LANES), lambda i: (i, 0)),
                   pl.BlockSpec((1, N_EXPERTS), lambda i: (0, 0))],
        out_shape=[jax.ShapeDtypeStruct((t, LANES), I32),
                   jax.ShapeDtypeStruct((t, LANES), F32),
                   jax.ShapeDtypeStruct((t, LANES), I32),
                   jax.ShapeDtypeStruct((1, N_EXPERTS), F32)],
        scratch_shapes=[pltpu.VMEM((1, N_EXPERTS), F32)],
        compiler_params=_cparams(("arbitrary",)),
        name="moe_router",
    )(x, w, b.reshape(1, N_EXPERTS))


def _for_each_row_copy(n, make_copy):
    def issue(jj, carry):
        for u in range(DMA_UNROLL):
            make_copy(jj * DMA_UNROLL + u, jj * (DMA_UNROLL // TOP_K) + u // TOP_K, u % TOP_K).start()
        return carry

    def drain(jj, carry):
        for u in range(DMA_UNROLL):
            make_copy(jj * DMA_UNROLL + u, jj * (DMA_UNROLL // TOP_K) + u // TOP_K, u % TOP_K).wait()
        return carry

    lax.fori_loop(0, n // DMA_UNROLL, issue, 0)
    lax.fori_loop(0, n // DMA_UNROLL, drain, 0)


def _dispatch_kernel(x_ref, dest_ref, xs_in_ref, xs_ref, sem):
    del xs_in_ref

    def row_copy(j, token, slot):
        del slot
        return pltpu.make_async_copy(x_ref.at[pl.ds(token, 1)], xs_ref.at[pl.ds(dest_ref[j], 1)], sem)

    _for_each_row_copy(dest_ref.shape[0], row_copy)


def _dispatch(x, dest_flat, xs_prev):
    t, dm = x.shape
    tb = min(MOE_TB, t)
    return pl.pallas_call(
        _dispatch_kernel,
        grid=(t // tb,),
        in_specs=[pl.BlockSpec((tb, dm), lambda i: (i, 0)),
                  pl.BlockSpec((tb * TOP_K,), lambda i: (i,), memory_space=pltpu.SMEM),
                  pl.BlockSpec(memory_space=pl.ANY)],
        out_specs=pl.BlockSpec(memory_space=pl.ANY),
        out_shape=jax.ShapeDtypeStruct(xs_prev.shape, xs_prev.dtype),
        scratch_shapes=[pltpu.SemaphoreType.DMA(())],
        input_output_aliases={2: 0},
        compiler_params=_cparams(("arbitrary",)),
        name="moe_dispatch",
    )(x, dest_flat, xs_prev)


def _expert_kernel(bexp_ref, nvalid_ref, x_ref, wg_ref, wu_ref, wd_ref, bg_ref, bu_ref, bd_ref, y_ref,
                   wg_s, wu_s, wd_s):
    i = pl.program_id(0)
    valid = i < nvalid_ref[0]
    changed = jnp.logical_or(i == 0, bexp_ref[i] != bexp_ref[jnp.maximum(i - 1, 0)])

    @pl.when(jnp.logical_and(valid, changed))
    def _():
        wg_s[...] = wg_ref[...].astype(BF16)
        wu_s[...] = wu_ref[...].astype(BF16)
        wd_s[...] = wd_ref[...].astype(BF16)

    @pl.when(valid)
    def _():
        x = x_ref[...].astype(BF16)
        glin = jnp.minimum(_dot(x, wg_s[...]) + bg_ref[...], SWIGLU_LIMIT)
        ulin = jnp.clip(_dot(x, wu_s[...]) + bu_ref[...], -SWIGLU_LIMIT, SWIGLU_LIMIT)
        act = (ulin + 1.0) * glin * _sigmoid(SWIGLU_ALPHA * glin)
        y_ref[...] = _dot(act.astype(BF16), wd_s[...]) + bd_ref[...]

    @pl.when(jnp.logical_not(valid))
    def _():
        y_ref[...] = jnp.zeros_like(y_ref)


def _experts(xs, bexp, nvalid, w_gate, b_gate, w_up, b_up, w_down, b_down):
    nrows, dm = xs.shape
    tm = MOE_TM
    nb = nrows // tm
    ff = w_gate.shape[-1]

    def row_map(i, be, nv):
        return (jnp.minimum(i, nv[0] - 1), 0)

    def w_map(i, be, nv):
        return (be[i], 0, 0)

    grid_spec = pltpu.PrefetchScalarGridSpec(
        num_scalar_prefetch=2,
        grid=(nb,),
        in_specs=[pl.BlockSpec((tm, dm), row_map),
                  pl.BlockSpec((None, dm, ff), w_map),
                  pl.BlockSpec((None, dm, ff), w_map),
                  pl.BlockSpec((None, ff, dm), w_map),
                  pl.BlockSpec((None, 1, ff), w_map),
                  pl.BlockSpec((None, 1, ff), w_map),
                  pl.BlockSpec((None, 1, dm), w_map)],
        out_specs=pl.BlockSpec((tm, dm), lambda i, be, nv: (i, 0)),
        scratch_shapes=[pltpu.VMEM((dm, ff), BF16), pltpu.VMEM((dm, ff), BF16),
                        pltpu.VMEM((ff, dm), BF16)])
    return pl.pallas_call(
        _expert_kernel,
        grid_spec=grid_spec,
        out_shape=jax.ShapeDtypeStruct((nrows, dm), F32),
        compiler_params=_cparams(("arbitrary",)),
        name="moe_experts",
    )(bexp, nvalid, xs, w_gate, w_up, w_down, b_gate.reshape(N_EXPERTS, 1, ff),
      b_up.reshape(N_EXPERTS, 1, ff), b_down.reshape(N_EXPERTS, 1, dm))


def _combine_kernel(h_ref, prob_ref, dest_ref, ys_ref, g_ref, b_ref, o_ref, obf_ref, ybuf, sem):
    def row_copy(j, token, slot):
        return pltpu.make_async_copy(ys_ref.at[pl.ds(dest_ref[j], 1)],
                                     ybuf.at[slot, pl.ds(token, 1)], sem)

    _for_each_row_copy(dest_ref.shape[0], row_copy)
    prob = prob_ref[...]
    ff = prob[:, 0:1] * ybuf[0]
    for kk in range(1, TOP_K):
        ff = ff + prob[:, kk:kk + 1] * ybuf[kk]
    y = _layer_norm_rows(DEEPNORM_ALPHA * h_ref[...] + ff, g_ref[...], b_ref[...])
    o_ref[...] = y
    obf_ref[...] = y.astype(BF16)


def _combine(h, prob, dest_flat, ys, g, b):
    t, dm = h.shape
    tb = min(MOE_TB, t)
    return pl.pallas_call(
        _combine_kernel,
        grid=(t // tb,),
        in_specs=[pl.BlockSpec((tb, dm), lambda i: (i, 0)),
                  pl.BlockSpec((tb, LANES), lambda i: (i, 0)),
                  pl.BlockSpec((tb * TOP_K,), lambda i: (i,), memory_space=pltpu.SMEM),
                  pl.BlockSpec(memory_space=pl.ANY),
                  pl.BlockSpec((1, dm), lambda i: (0, 0)),
                  pl.BlockSpec((1, dm), lambda i: (0, 0))],
        out_specs=[pl.BlockSpec((tb, dm), lambda i: (i, 0)),
                   pl.BlockSpec((tb, dm), lambda i: (i, 0))],
        out_shape=[jax.ShapeDtypeStruct((t, dm), F32), jax.ShapeDtypeStruct((t, dm), BF16)],
        scratch_shapes=[pltpu.VMEM((TOP_K, tb, dm), F32), pltpu.SemaphoreType.DMA(())],
        compiler_params=_cparams(("arbitrary",)),
        name="moe_combine",
    )(h, prob, dest_flat, ys, g.reshape(1, dm), b.reshape(1, dm))


def _moe_rows(t):
    return ((t * TOP_K) // MOE_TM + N_EXPERTS) * MOE_TM


def _moe_layer(h, xs_prev, w_router, b_router, w_gate, b_gate, w_up, b_up, w_down, b_down, ln_g, ln_b):
    t, dm = h.shape
    tm = MOE_TM
    idx, prob, pre, counts = _router(h, w_router, b_router)
    counts = counts.reshape(N_EXPERTS).astype(I32)
    nblk = (counts + tm - 1) // tm
    blk_end = jnp.cumsum(nblk)
    gstart = ((blk_end - nblk) * tm).astype(I32)
    nvalid = blk_end[-1:].astype(I32)
    nb = _moe_rows(t) // tm
    blk = jnp.minimum(jnp.arange(nb, dtype=I32), nvalid[0] - 1)
    bexp = jnp.sum((blk[:, None] >= blk_end[None, :]).astype(I32), axis=1).astype(I32)
    idx4, pre4 = idx[:, :TOP_K], pre[:, :TOP_K]
    hot = idx4[:, :, None] == jnp.arange(N_EXPERTS, dtype=I32)[None, None, :]
    dest = pre4 + jnp.sum(jnp.where(hot, gstart[None, None, :], 0), axis=-1)
    dest_flat = dest.reshape(t * TOP_K).astype(I32)
    xs = _dispatch(h, dest_flat, xs_prev)
    ys = _experts(xs, bexp, nvalid, w_gate, b_gate, w_up, b_up, w_down, b_down)
    hn, hnb = _combine(h, prob, dest_flat, ys, ln_g, ln_b)
    return hn, hnb, xs


def _even_layer(h, hb, w_in, w_out, s5_b_re, s5_b_im, s5_c_re, s5_c_im, s5_d, s5_a_re, s5_a_im,
                s5_log_dt, s5_w_glu, s5_b_glu, ln_g, ln_b):
    t = h.shape[0]
    w_in_p = jnp.concatenate([w_in[:, :EVEN_QKVG], _s5_channel_major(w_in[:, EVEN_QKVG:], 1)], axis=1)
    w_glu_p = _s5_channel_major(_s5_channel_major(s5_w_glu, 0), 1)
    b_glu_p = _s5_channel_major(s5_b_glu, 0)
    w_out_p = jnp.concatenate([w_out[:RET_WIDTH], _s5_channel_major(w_out[RET_WIDTH:], 0)], axis=0)
    p = _matmul(hb, w_in_p.astype(BF16), w_in.shape[1], BF16, 1024, 512)
    o_ret = _retention(p, t)
    y = _s5(p[:, EVEN_QKVG:], s5_b_re, s5_b_im, s5_c_re, s5_c_im, s5_d, s5_a_re, s5_a_im, s5_log_dt)
    y = _matmul_glu(y, w_glu_p.astype(BF16), b_glu_p)
    mixed = jnp.concatenate([o_ret, y], axis=-1)
    return _matmul_ln(mixed, w_out_p.astype(BF16), h, ln_g, ln_b)


def _odd_layer(h, hb, w_in, w_out, conv_w, a_log, dt_bias, norm_w, ln_g, ln_b):
    t = h.shape[0]
    w16 = w_in.astype(BF16)
    p = _matmul(hb, w16, GDN_MAIN, BF16, 1024, 512)
    ab = _matmul(hb, w_in[:, GDN_MAIN:].astype(BF16), 2 * GDN_HEADS, F32, 1024, 2 * GDN_HEADS)
    o = _gated_deltanet(p, ab, conv_w, a_log, dt_bias, norm_w, t)
    return _matmul_ln(o, w_out.astype(BF16), h, ln_g, ln_b)


def kernel(x, ev_w_in, ev_w_out, ev_s5_b_re, ev_s5_b_im, ev_s5_c_re, ev_s5_c_im, ev_s5_d, ev_s5_a_re,
           ev_s5_a_im, ev_s5_log_dt, ev_s5_w_glu, ev_s5_b_glu, ev_ln_g, ev_ln_b, od_w_in, od_w_out,
           od_conv_w, od_a_log, od_dt_bias, od_norm_w, od_ln_g, od_ln_b, moe_w_router, moe_b_router,
           moe_w_gate, moe_b_gate, moe_w_up, moe_b_up, moe_w_down, moe_b_down, moe_ln_g, moe_ln_b):
    bsz, seq, dm = x.shape
    outs = []
    xs = jnp.zeros((_moe_rows(seq), dm), F32)
    for bi in range(bsz):
        h = x[bi]
        hb = h.astype(BF16)
        for layer in range(DEPTH):
            p = layer // 2
            if layer % 2 == 0:
                h, hb = _even_layer(h, hb, ev_w_in[p], ev_w_out[p], ev_s5_b_re[p], ev_s5_b_im[p],
                                    ev_s5_c_re[p], ev_s5_c_im[p], ev_s5_d[p], ev_s5_a_re[p],
                                    ev_s5_a_im[p], ev_s5_log_dt[p], ev_s5_w_glu[p], ev_s5_b_glu[p],
                                    ev_ln_g[p], ev_ln_b[p])
            else:
                h, hb = _odd_layer(h, hb, od_w_in[p], od_w_out[p], od_conv_w[p], od_a_log[p],
                                   od_dt_bias[p], od_norm_w[p], od_ln_g[p], od_ln_b[p])
            h, hb, xs = _moe_layer(h, xs, moe_w_router[layer], moe_b_router[layer], moe_w_gate[layer],
                                   moe_b_gate[layer], moe_w_up[layer], moe_b_up[layer],
                                   moe_w_down[layer], moe_b_down[layer], moe_ln_g[layer],
                                   moe_ln_b[layer])
        outs.append(h)
    return jnp.stack(outs, axis=0)
```

```python
import functools
import math

import jax
import jax.numpy as jnp
from jax import lax
from jax.experimental import pallas as pl
from jax.experimental.pallas import tpu as pltpu

F32 = jnp.float32
BF16 = jnp.bfloat16
I32 = jnp.int32
U32 = jnp.uint32
HIGHEST = lax.Precision.HIGHEST

D_MODEL = 4096
DEPTH = 4
CHUNK = 64
RET_HEADS = 8
RET_WIDTH = D_MODEL // 2
RET_DV = RET_WIDTH // RET_HEADS
RET_DK = RET_DV // 2
ROPE_BASE = 10000.0
S5_WIDTH = D_MODEL - RET_WIDTH
S5_GROUP = 16
S5_GROUPS = S5_WIDTH // S5_GROUP
S5_STATE = 64
EVEN_QKVG = 2 * RET_HEADS * RET_DK + 2 * RET_WIDTH
GDN_HEADS = 16
GDN_DV = D_MODEL // GDN_HEADS
GDN_DK = GDN_DV // 2
CONV_K = 4
GDN_MAIN = 2 * GDN_HEADS * GDN_DK + 2 * GDN_HEADS * GDN_DV
N_EXPERTS = 32
TOP_K = 4
EXPERT_FF = D_MODEL // 16
SWIGLU_LIMIT = 7.0
SWIGLU_ALPHA = 1.702
DEEPNORM_ALPHA = (2 * DEPTH) ** 0.25
LN_EPS = 1e-5
NORM_EPS = 1e-6

V7X_VMEM_LIMIT_BYTES = 56 * 1024 * 1024
LANES = 128
BF16_SUBLANES = 16
S5_SUB = 16
RET_BLOCK = 256
GDN_BLOCK = 256
GDN_PREP_HEADS = 2
MOE_TM = 256
MOE_TB = 128
DMA_UNROLL = 8
NEG_BIG = -3.0e38


def _cparams(sem, vmem=V7X_VMEM_LIMIT_BYTES):
    return pltpu.CompilerParams(dimension_semantics=sem, vmem_limit_bytes=vmem)


def _dot(a, b):
    return jnp.dot(a, b, preferred_element_type=F32)


def _dot_nt(a, b):
    return lax.dot_general(a, b, (((1,), (1,)), ((), ())), preferred_element_type=F32)


def _dot_tn(a, b):
    return lax.dot_general(a, b, (((0,), (0,)), ((), ())), preferred_element_type=F32)


def _dot_hi(a, b):
    return jnp.dot(a, b, preferred_element_type=F32, precision=HIGHEST)


def _sigmoid(x):
    return 1.0 / (1.0 + jnp.exp(-x))


def _mm_kernel(x_ref, w_ref, o_ref):
    o_ref[...] = _dot(x_ref[...].astype(BF16), w_ref[...]).astype(o_ref.dtype)


def _matmul(x, w, n, out_dtype, tm, tn):
    m, k = x.shape
    tm, tn = min(tm, m), min(tn, n)
    return pl.pallas_call(
        _mm_kernel,
        grid=(m // tm, n // tn),
        in_specs=[pl.BlockSpec((tm, k), lambda i, j: (i, 0)),
                  pl.BlockSpec((k, tn), lambda i, j: (0, j))],
        out_specs=pl.BlockSpec((tm, tn), lambda i, j: (i, j)),
        out_shape=jax.ShapeDtypeStruct((m, n), out_dtype),
        compiler_params=_cparams(("parallel", "parallel")),
        name="proj_matmul",
    )(x, w)


def _pack_bf16_pair(lo, hi):
    lo_bits = lax.bitcast_convert_type(lo.astype(BF16).astype(F32), U32)
    hi_bits = lax.bitcast_convert_type(hi.astype(BF16).astype(F32), U32)
    return jnp.right_shift(lo_bits, jnp.uint32(16)) | (hi_bits & jnp.uint32(0xFFFF0000))


def _unpack_bf16_pair(word):
    lo = lax.bitcast_convert_type(jnp.left_shift(word, jnp.uint32(16)), F32)
    hi = lax.bitcast_convert_type(word & jnp.uint32(0xFFFF0000), F32)
    return lo, hi


def _mm_ln_kernel(x_ref, w_ref, res_ref, g_ref, b_ref, o_ref, hp_ref, z_ref, *, nj):
    j = pl.program_id(1)
    z_ref[j] = DEEPNORM_ALPHA * res_ref[...] + _dot(x_ref[...].astype(BF16), w_ref[...])

    @pl.when(j == nj - 1)
    def _():
        tn = z_ref.shape[2]
        n = nj * tn
        tot = z_ref[0]
        for jj in range(1, nj):
            tot = tot + z_ref[jj]
        mu = jnp.sum(tot, axis=-1, keepdims=True) * (1.0 / n)
        sq = jnp.square(z_ref[0] - mu)
        for jj in range(1, nj):
            sq = sq + jnp.square(z_ref[jj] - mu)
        rs = lax.rsqrt(jnp.sum(sq, axis=-1, keepdims=True) * (1.0 / n) + LN_EPS)
        half = nj // 2
        for jj in range(half):
            ca = slice(jj * tn, (jj + 1) * tn)
            cb = slice((jj + half) * tn, (jj + half + 1) * tn)
            ya = (z_ref[jj] - mu) * rs * g_ref[:, ca] + b_ref[:, ca]
            yb = (z_ref[jj + half] - mu) * rs * g_ref[:, cb] + b_ref[:, cb]
            o_ref[:, ca] = ya
            o_ref[:, cb] = yb
            hp_ref[:, ca] = _pack_bf16_pair(ya, yb)


def _matmul_ln(x, w, res, g, b, tm=512, tn=512):
    m, kdim = x.shape
    n = w.shape[1]
    tm, tn = min(tm, m), min(tn, n)
    nj = n // tn
    return pl.pallas_call(
        functools.partial(_mm_ln_kernel, nj=nj),
        grid=(m // tm, nj),
        in_specs=[pl.BlockSpec((tm, kdim), lambda i, j: (i, 0)),
                  pl.BlockSpec((kdim, tn), lambda i, j: (0, j)),
                  pl.BlockSpec((tm, tn), lambda i, j: (i, j)),
                  pl.BlockSpec((1, n), lambda i, j: (0, 0)),
                  pl.BlockSpec((1, n), lambda i, j: (0, 0))],
        out_specs=[pl.BlockSpec((tm, n), lambda i, j: (i, 0)),
                   pl.BlockSpec((tm, n // 2), lambda i, j: (i, 0))],
        out_shape=[jax.ShapeDtypeStruct((m, n), F32), jax.ShapeDtypeStruct((m, n // 2), U32)],
        scratch_shapes=[pltpu.VMEM((nj, tm, tn), F32)],
        compiler_params=_cparams(("parallel", "arbitrary")),
        name="proj_layernorm",
    )(x, w, res, g.reshape(1, n), b.reshape(1, n))


def _mm_glu_kernel(x_ref, yt_ref, w_ref, b_ref, o_ref):
    z = _dot(x_ref[...], w_ref[...]) + b_ref[...]
    o_ref[...] = (yt_ref[...].astype(F32) * _sigmoid(z)).astype(o_ref.dtype)


def _matmul_glu(y, w, b, tm=1024, tn=512):
    m, k = y.shape
    n = w.shape[1]
    tm, tn = min(tm, m), min(tn, n)
    return pl.pallas_call(
        _mm_glu_kernel,
        grid=(m // tm, n // tn),
        in_specs=[pl.BlockSpec((tm, k), lambda i, j: (i, 0)),
                  pl.BlockSpec((tm, tn), lambda i, j: (i, j)),
                  pl.BlockSpec((k, tn), lambda i, j: (0, j)),
                  pl.BlockSpec((1, tn), lambda i, j: (0, j))],
        out_specs=pl.BlockSpec((tm, tn), lambda i, j: (i, j)),
        out_shape=jax.ShapeDtypeStruct((m, n), BF16),
        compiler_params=_cparams(("parallel", "parallel")),
        name="s5_glu",
    )(y, y, w, b.reshape(1, n))


def _ret_kernel(bdec_ref, q_ref, k_ref, v_ref, gate_ref, cos_ref, sin_ref, dmask_ref, qdec_ref,
                kdec_ref, o_ref, s_ref):
    h = pl.program_id(0)

    @pl.when(pl.program_id(1) == 0)
    def _():
        s_ref[...] = jnp.zeros_like(s_ref)

    cos = cos_ref[...]
    sin = sin_ref[...]
    q = q_ref[...].astype(F32)
    k = k_ref[...].astype(F32)
    half = RET_DK // 2
    qr = q * cos + pltpu.roll(q, half, 1) * sin
    kr = (k * cos + pltpu.roll(k, half, 1) * sin) * (RET_DK ** -0.5)
    v = v_ref[...].astype(BF16)
    scores = _dot_nt(qr.astype(BF16), kr.astype(BF16)) * dmask_ref[0]
    o = _dot(scores.astype(BF16), v)
    state = s_ref[...]
    o = o + _dot((qr * qdec_ref[0]).astype(BF16), state.astype(BF16))
    s_ref[...] = state * bdec_ref[h] + _dot_tn((kr * kdec_ref[0]).astype(BF16), v)
    mu = jnp.mean(o, axis=-1, keepdims=True)
    oc = o - mu
    var = jnp.mean(oc * oc, axis=-1, keepdims=True)
    on = oc * lax.rsqrt(var + NORM_EPS)
    gate = gate_ref[...].astype(F32)
    o_ref[...] = (on * (gate * _sigmoid(gate))).astype(o_ref.dtype)


def _retention_tables(t, bt):
    half = RET_DK // 2
    inv_freq = jnp.power(ROPE_BASE, -jnp.arange(half, dtype=F32) / half)
    ang = jnp.arange(t, dtype=F32)[:, None] * inv_freq[None, :]
    cos, sin = jnp.cos(ang), jnp.sin(ang)
    cos2 = jnp.concatenate([cos, cos], axis=-1)
    sin2 = jnp.concatenate([-sin, sin], axis=-1)
    log_gamma = jnp.log1p(-jnp.exp2(-5.0 - jnp.arange(RET_HEADS, dtype=F32)))
    pos = jnp.arange(bt, dtype=F32)
    dist = jnp.abs(pos[:, None] - pos[None, :])
    chunk_id = jnp.arange(bt) // CHUNK
    visible = chunk_id[None, :] <= chunk_id[:, None]
    dmask = jnp.where(visible[None], jnp.exp(log_gamma[:, None, None] * dist[None]), 0.0)
    qdec = jnp.exp(log_gamma[:, None] * (pos + 1.0))
    kdec = jnp.exp(log_gamma[:, None] * (bt - 1.0 - pos))
    qdec = jnp.broadcast_to(qdec[:, :, None], (RET_HEADS, bt, RET_DK))
    kdec = jnp.broadcast_to(kdec[:, :, None], (RET_HEADS, bt, RET_DK))
    bdec = jnp.exp(log_gamma * bt)
    return cos2, sin2, dmask, qdec, kdec, bdec


def _retention(p, t):
    bt = min(RET_BLOCK, t)
    cos2, sin2, dmask, qdec, kdec, bdec = _retention_tables(t, bt)
    kq = RET_HEADS
    kv = 2 * RET_HEADS * RET_DK // RET_DV
    kg = kv + RET_HEADS
    grid_spec = pltpu.PrefetchScalarGridSpec(
        num_scalar_prefetch=0,
        grid=(RET_HEADS, t // bt),
        in_specs=[pl.BlockSpec(memory_space=pltpu.SMEM),
                  pl.BlockSpec((bt, RET_DK), lambda h, i: (i, h)),
                  pl.BlockSpec((bt, RET_DK), lambda h, i: (i, kq + h)),
                  pl.BlockSpec((bt, RET_DV), lambda h, i: (i, kv + h)),
                  pl.BlockSpec((bt, RET_DV), lambda h, i: (i, kg + h)),
                  pl.BlockSpec((bt, RET_DK), lambda h, i: (i, 0)),
                  pl.BlockSpec((bt, RET_DK), lambda h, i: (i, 0)),
                  pl.BlockSpec((1, bt, bt), lambda h, i: (h, 0, 0)),
                  pl.BlockSpec((1, bt, RET_DK), lambda h, i: (h, 0, 0)),
                  pl.BlockSpec((1, bt, RET_DK), lambda h, i: (h, 0, 0))],
        out_specs=pl.BlockSpec((bt, RET_DV), lambda h, i: (i, h)),
        scratch_shapes=[pltpu.VMEM((RET_DK, RET_DV), F32)])
    return pl.pallas_call(
        _ret_kernel,
        grid_spec=grid_spec,
        out_shape=jax.ShapeDtypeStruct((t, RET_WIDTH), BF16),
        compiler_params=_cparams(("parallel", "arbitrary")),
        name="retention",
    )(bdec, p, p, p, p, cos2, sin2, dmask, qdec, kdec)


def _gelu_tanh(x):
    c = math.sqrt(2.0 / math.pi)
    return 0.5 * x * (1.0 + jnp.tanh(c * (x + 0.044715 * (x * x * x))))


def _s5_kernel(u_ref, m_ref, bm_ref, cm_ref, pw_ref, d_ref, y_ref, *, nlev):
    u = u_ref[0].astype(F32)
    ub = u.astype(BF16)
    nc = u.shape[0]
    hstate = _dot(ub, bm_ref[0])
    row = lax.broadcasted_iota(I32, hstate.shape, 0)
    pw = pw_ref[0]
    for lev in range(nlev):
        s = 1 << lev
        hs = jnp.where(row >= s, pltpu.roll(hstate, s, 0), 0.0)
        hstate = (hstate + pw[2 * lev:2 * lev + 1, :] * hs
                  + pw[2 * lev + 1:2 * lev + 2, :] * pltpu.roll(hs, S5_STATE, 1))
    if nc > 1:
        hprev = jnp.where(row >= 1, pltpu.roll(hstate, 1, 0), 0.0)
    else:
        hprev = jnp.zeros_like(hstate)
    y = _dot(ub, m_ref[0]) + _dot(hprev.astype(BF16), cm_ref[0]) + d_ref[0] * u
    y_ref[0] = _gelu_tanh(y).astype(y_ref.dtype)


def _s5_tables(b_re, b_im, c_re, c_im, d, a_re, a_im, log_dt, nc):
    g, p, i = S5_GROUPS, S5_STATE, S5_GROUP
    l = S5_SUB
    dt = jnp.exp(log_dt)[:, None]
    lam_re = jnp.minimum(a_re, -1e-4)
    lam_im = a_im
    mag = jnp.exp(lam_re * dt)
    ab_re = mag * jnp.cos(lam_im * dt)
    ab_im = mag * jnp.sin(lam_im * dt)
    den = jnp.square(lam_re) + jnp.square(lam_im)
    f_re = ((ab_re - 1.0) * lam_re + ab_im * lam_im) / den
    f_im = (ab_im * lam_re - (ab_re - 1.0) * lam_im) / den
    bb_re = f_re[..., None] * b_re - f_im[..., None] * b_im
    bb_im = f_re[..., None] * b_im + f_im[..., None] * b_re

    def apow(n):
        n = jnp.asarray(n, F32)[..., None, None]
        m = jnp.exp(n * (lam_re * dt))
        return m * jnp.cos(n * (lam_im * dt)), m * jnp.sin(n * (lam_im * dt))

    wr, wi = apow(jnp.arange(l))
    xr = wr[..., None] * bb_re[None] - wi[..., None] * bb_im[None]
    xi = wr[..., None] * bb_im[None] + wi[..., None] * bb_re[None]
    kern = (jnp.einsum('gip,jgpk->gjik', c_re, xr, precision=HIGHEST)
            - jnp.einsum('gip,jgpk->gjik', c_im, xi, precision=HIGHEST))
    lag = jnp.arange(l)[None, :] - jnp.arange(l)[:, None]
    toe = kern[:, jnp.clip(lag, 0, l - 1)]
    toe = jnp.where((lag >= 0)[None, :, :, None, None], toe, 0.0)
    m_mat = toe.transpose(0, 1, 4, 2, 3).reshape(g, l * i, l * i)
    wr, wi = apow(l - 1 - jnp.arange(l))
    br = wr[..., None] * bb_re[None] - wi[..., None] * bb_im[None]
    bi = wr[..., None] * bb_im[None] + wi[..., None] * bb_re[None]
    bm = jnp.concatenate([br, bi], axis=2)
    bm = bm.transpose(1, 0, 3, 2).reshape(g, l * i, 2 * p)
    wr, wi = apow(jnp.arange(l) + 1.0)
    cr = (c_re.transpose(0, 2, 1)[None] * wr[..., None]
          - c_im.transpose(0, 2, 1)[None] * wi[..., None])
    ci = (-c_re.transpose(0, 2, 1)[None] * wi[..., None]
          - c_im.transpose(0, 2, 1)[None] * wr[..., None])
    cm = jnp.concatenate([cr, ci], axis=2)
    cm = cm.transpose(1, 2, 0, 3).reshape(g, 2 * p, l * i)
    nlev = max(1, int(math.ceil(math.log2(nc)))) if nc > 1 else 0
    if nlev:
        wr, wi = apow(float(l) * (2.0 ** jnp.arange(nlev)))
        pw = jnp.stack([jnp.concatenate([wr, wr], axis=-1),
                        jnp.concatenate([-wi, wi], axis=-1)], axis=1)
        pw = pw.reshape(2 * nlev, g, 2 * p).transpose(1, 0, 2)
    else:
        pw = jnp.zeros((g, 2, 2 * p), F32)
    dd = jnp.tile(d.reshape(g, 1, i), (1, l, 1)).reshape(g, 1, l * i)
    return m_mat.astype(BF16), bm.astype(BF16), cm.astype(BF16), pw, dd, nlev


def _s5_channel_major(w, axis):
    shape = w.shape
    w = w.reshape(shape[:axis] + (S5_GROUPS, S5_GROUP) + shape[axis + 1:])
    return jnp.swapaxes(w, axis, axis + 1).reshape(shape)


def _s5(u, b_re, b_im, c_re, c_im, d, a_re, a_im, log_dt):
    t = u.shape[0]
    g, l, i = S5_GROUPS, S5_SUB, S5_GROUP
    nc = t // l
    m_mat, bm, cm, pw, dd, nlev = _s5_tables(b_re, b_im, c_re, c_im, d, a_re, a_im, log_dt, nc)
    ug = u.reshape(t * i, g).T.reshape(g, nc, l * i)
    npw = pw.shape[1]
    y = pl.pallas_call(
        functools.partial(_s5_kernel, nlev=nlev),
        grid=(g,),
        in_specs=[pl.BlockSpec((1, nc, l * i), lambda j: (j, 0, 0)),
                  pl.BlockSpec((1, l * i, l * i), lambda j: (j, 0, 0)),
                  pl.BlockSpec((1, l * i, 2 * S5_STATE), lambda j: (j, 0, 0)),
                  pl.BlockSpec((1, 2 * S5_STATE, l * i), lambda j: (j, 0, 0)),
                  pl.BlockSpec((1, npw, 2 * S5_STATE), lambda j: (j, 0, 0)),
                  pl.BlockSpec((1, 1, l * i), lambda j: (j, 0, 0))],
        out_specs=pl.BlockSpec((1, nc, l * i), lambda j: (j, 0, 0)),
        out_shape=jax.ShapeDtypeStruct((g, nc, l * i), BF16),
        compiler_params=_cparams(("parallel",)),
        name="s5_scan",
    )(ug, m_mat, bm, cm, pw, dd)
    return y.reshape(g, t * i).T.reshape(t, g * i)


def _causal_conv_silu(x, halo, w):
    row = lax.broadcasted_iota(I32, x.shape, 0)
    y = w[CONV_K - 1:CONV_K, :] * x
    for j in range(1, CONV_K):
        head = jnp.concatenate([pltpu.roll(halo, j, 0)] * (x.shape[0] // halo.shape[0]), axis=0)
        shifted = jnp.where(row >= j, pltpu.roll(x, j, 0), head)
        y = y + w[CONV_K - 1 - j:CONV_K - j, :] * shifted
    return y * _sigmoid(y)


def _softplus(x):
    return jnp.maximum(x, 0.0) + jnp.log(1.0 + jnp.exp(-jnp.abs(x)))


def _gdn_prep_kernel(nega_ref, dtb_ref, q_ref, k_ref, v_ref, qh_ref, kh_ref, vh_ref, wq_ref, wk_ref,
                     wv_ref, ab_ref, abt_ref, wvo_ref, kco_ref, qgo_ref, kdo_ref, ato_ref, ego_ref):
    hp = pl.program_id(0)
    keep = (pl.program_id(1) > 0).astype(F32)
    q_all = _causal_conv_silu(q_ref[...].astype(F32), qh_ref[...].astype(F32) * keep, wq_ref[...])
    k_all = _causal_conv_silu(k_ref[...].astype(F32), kh_ref[...].astype(F32) * keep, wk_ref[...])
    v_all = _causal_conv_silu(v_ref[...].astype(F32), vh_ref[...].astype(F32) * keep, wv_ref[...])
    bt = q_all.shape[0]
    ab = ab_ref[...]
    lane = lax.broadcasted_iota(I32, ab.shape, 1)
    abt = abt_ref[...]
    sub = lax.broadcasted_iota(I32, abt.shape, 0)

    bi = lax.broadcasted_iota(I32, (bt, bt), 0)
    bj = lax.broadcasted_iota(I32, (bt, bt), 1)
    same = (bi // CHUNK) == (bj // CHUNK)
    lower = jnp.logical_and(same, bi >= bj)
    strict = jnp.logical_and(same, bi > bj)
    upper = jnp.logical_and(same, bi <= bj)
    eye = (bi == bj).astype(F32)

    for hh in range(GDN_PREP_HEADS):
        h = hp * GDN_PREP_HEADS + hh
        q = q_all[:, hh * GDN_DK:(hh + 1) * GDN_DK]
        k = k_all[:, hh * GDN_DK:(hh + 1) * GDN_DK]
        v = v_all[:, hh * GDN_DV:(hh + 1) * GDN_DV]
        q = q * lax.rsqrt(jnp.sum(q * q, axis=-1, keepdims=True) + NORM_EPS) * (GDN_DK ** -0.5)
        k = k * lax.rsqrt(jnp.sum(k * k, axis=-1, keepdims=True) + NORM_EPS)
        neg_a = nega_ref[h]
        dtb = dtb_ref[h]
        a_col = jnp.sum(jnp.where(lane == h, ab, 0.0), axis=1, keepdims=True)
        b_col = jnp.sum(jnp.where(lane == GDN_HEADS + h, ab, 0.0), axis=1, keepdims=True)
        a_row = jnp.sum(jnp.where(sub == h, abt, 0.0), axis=0, keepdims=True)
        g_col = neg_a * _softplus(a_col + dtb)
        g_row = neg_a * _softplus(a_row + dtb)
        beta = _sigmoid(b_col)
        gcum_col = jnp.sum(jnp.where(lower, g_row, 0.0), axis=1, keepdims=True)
        gcum_row = jnp.sum(jnp.where(upper, g_col, 0.0), axis=0, keepdims=True)
        gtot_col = jnp.sum(jnp.where(same, g_row, 0.0), axis=1, keepdims=True)
        decay = jnp.where(lower, jnp.exp(jnp.where(lower, gcum_col - gcum_row, 0.0)), 0.0)
        k_beta = k * beta
        v_beta = v * beta
        k16 = k.astype(BF16)
        lmat = jnp.where(strict, _dot_nt(k_beta.astype(BF16), k16) * decay, 0.0)
        tinv = eye - lmat
        l16 = lmat.astype(BF16)
        pw = _dot(l16, l16)
        n = 2
        while True:
            tinv = tinv + _dot(tinv.astype(BF16), pw.astype(BF16))
            n *= 2
            if n >= CHUNK:
                break
            p16 = pw.astype(BF16)
            pw = _dot(p16, p16)
        t16 = tinv.astype(BF16)
        eg = jnp.exp(gcum_col)
        w_v = _dot(t16, v_beta.astype(BF16))
        k_cum = _dot(t16, (k_beta * eg).astype(BF16))
        attn = jnp.where(lower, _dot_nt(q.astype(BF16), k16) * decay, 0.0)
        folded = attn[:, :LANES]
        for c in range(1, bt // LANES):
            folded = folded + attn[:, c * LANES:(c + 1) * LANES]
        folded = folded + pltpu.roll(folded, CHUNK, 1)
        wvo_ref[:, hh * GDN_DV:(hh + 1) * GDN_DV] = w_v.astype(BF16)
        kco_ref[:, hh * GDN_DK:(hh + 1) * GDN_DK] = k_cum.astype(BF16)
        qgo_ref[:, hh * GDN_DK:(hh + 1) * GDN_DK] = (q * eg).astype(BF16)
        kdo_ref[:, hh * GDN_DK:(hh + 1) * GDN_DK] = (k * jnp.exp(gtot_col - gcum_col)).astype(BF16)
        ato_ref[:, hh * LANES:(hh + 1) * LANES] = folded.astype(BF16)
        ego_ref[:, hh * LANES:(hh + 1) * LANES] = jnp.broadcast_to(jnp.exp(gtot_col), (bt, LANES))


def _gdn_scan_kernel(wv_ref, kc_ref, qg_ref, kd_ref, at_ref, eg_ref, gate_ref, nw_ref, o_ref, s_ref):
    @pl.when(pl.program_id(0) == 0)
    def _():
        s_ref[...] = jnp.zeros_like(s_ref)

    nw = nw_ref[...]

    def chunk_step(c, carry):
        r0 = pl.multiple_of(c * CHUNK, CHUNK)
        rows = pl.ds(r0, CHUNK)
        for h in range(GDN_HEADS):
            ck = slice(h * GDN_DK, (h + 1) * GDN_DK)
            cv = slice(h * GDN_DV, (h + 1) * GDN_DV)
            state = s_ref[h]
            s16 = state.astype(BF16)
            v_new = wv_ref[rows, cv].astype(F32) - _dot(kc_ref[rows, ck], s16)
            vn16 = v_new.astype(BF16)
            attn = at_ref[rows, h * LANES:h * LANES + CHUNK]
            o = _dot(qg_ref[rows, ck], s16) + _dot(attn, vn16)
            eg = eg_ref[pl.ds(r0, 1), h * LANES:h * LANES + 1]
            s_ref[h] = state * eg + _dot_tn(kd_ref[rows, ck], vn16)
            rms = lax.rsqrt(jnp.mean(o * o, axis=-1, keepdims=True) + NORM_EPS)
            gc = gate_ref[rows, cv].astype(F32)
            o_ref[rows, cv] = (o * rms * nw * (gc * _sigmoid(gc))).astype(o_ref.dtype)
        return carry

    lax.fori_loop(0, wv_ref.shape[0] // CHUNK, chunk_step, 0)


def _gated_deltanet(p, ab, conv_w, a_log, dt_bias, norm_w, t):
    bt = min(GDN_BLOCK, t)
    hq, hg = GDN_HEADS, GDN_PREP_HEADS
    wk_, wv_ = hg * GDN_DK, hg * GDN_DV
    kk = hq * GDN_DK // wk_
    kv = 2 * hq * GDN_DK // wv_
    hb = bt // BF16_SUBLANES

    def halo_map(off):
        return lambda g, i: (jnp.maximum(i * hb - 1, 0), off + g)

    prep_spec = pltpu.PrefetchScalarGridSpec(
        num_scalar_prefetch=0,
        grid=(hq // hg, t // bt),
        in_specs=[pl.BlockSpec(memory_space=pltpu.SMEM),
                  pl.BlockSpec(memory_space=pltpu.SMEM),
                  pl.BlockSpec((bt, wk_), lambda g, i: (i, g)),
                  pl.BlockSpec((bt, wk_), lambda g, i: (i, kk + g)),
                  pl.BlockSpec((bt, wv_), lambda g, i: (i, kv + g)),
                  pl.BlockSpec((BF16_SUBLANES, wk_), halo_map(0)),
                  pl.BlockSpec((BF16_SUBLANES, wk_), halo_map(kk)),
                  pl.BlockSpec((BF16_SUBLANES, wv_), halo_map(kv)),
                  pl.BlockSpec((CONV_K, wk_), lambda g, i: (0, g)),
                  pl.BlockSpec((CONV_K, wk_), lambda g, i: (0, kk + g)),
                  pl.BlockSpec((CONV_K, wv_), lambda g, i: (0, kv + g)),
                  pl.BlockSpec((bt, 2 * hq), lambda g, i: (i, 0)),
                  pl.BlockSpec((2 * hq, bt), lambda g, i: (0, i))],
        out_specs=[pl.BlockSpec((bt, wv_), lambda g, i: (i, g)),
                   pl.BlockSpec((bt, wk_), lambda g, i: (i, g)),
                   pl.BlockSpec((bt, wk_), lambda g, i: (i, g)),
                   pl.BlockSpec((bt, wk_), lambda g, i: (i, g)),
                   pl.BlockSpec((bt, hg * LANES), lambda g, i: (i, g)),
                   pl.BlockSpec((bt, hg * LANES), lambda g, i: (i, g))])
    w_v, k_cum, q_g, k_d, attn, eg = pl.pallas_call(
        _gdn_prep_kernel,
        grid_spec=prep_spec,
        out_shape=[jax.ShapeDtypeStruct((t, hq * GDN_DV), BF16),
                   jax.ShapeDtypeStruct((t, hq * GDN_DK), BF16),
                   jax.ShapeDtypeStruct((t, hq * GDN_DK), BF16),
                   jax.ShapeDtypeStruct((t, hq * GDN_DK), BF16),
                   jax.ShapeDtypeStruct((t, hq * LANES), BF16),
                   jax.ShapeDtypeStruct((t, hq * LANES), F32)],
        compiler_params=_cparams(("parallel", "parallel")),
        name="gdn_prep",
    )(-jnp.exp(a_log), dt_bias, p, p, p, p, p, p, conv_w, conv_w, conv_w, ab, ab.T)

    gate_blk = (2 * hq * GDN_DK + hq * GDN_DV) // (hq * GDN_DV)
    return pl.pallas_call(
        _gdn_scan_kernel,
        grid=(t // bt,),
        in_specs=[pl.BlockSpec((bt, hq * GDN_DV), lambda i: (i, 0)),
                  pl.BlockSpec((bt, hq * GDN_DK), lambda i: (i, 0)),
                  pl.BlockSpec((bt, hq * GDN_DK), lambda i: (i, 0)),
                  pl.BlockSpec((bt, hq * GDN_DK), lambda i: (i, 0)),
                  pl.BlockSpec((bt, hq * LANES), lambda i: (i, 0)),
                  pl.BlockSpec((bt, hq * LANES), lambda i: (i, 0)),
                  pl.BlockSpec((bt, hq * GDN_DV), lambda i: (i, gate_blk)),
                  pl.BlockSpec((1, GDN_DV), lambda i: (0, 0))],
        out_specs=pl.BlockSpec((bt, hq * GDN_DV), lambda i: (i, 0)),
        out_shape=jax.ShapeDtypeStruct((t, hq * GDN_DV), BF16),
        scratch_shapes=[pltpu.VMEM((hq, GDN_DK, GDN_DV), F32)],
        compiler_params=_cparams(("arbitrary",)),
        name="gdn_scan",
    )(w_v, k_cum, q_g, k_d, attn, eg, p, norm_w.reshape(1, GDN_DV))


def _router_kernel(x_ref, w_ref, b_ref, idx_ref, prob_ref, pre_ref, cnt_ref, run_ref):
    @pl.when(pl.program_id(0) == 0)
    def _():
        run_ref[...] = jnp.zeros_like(run_ref)

    x = x_ref[...]
    w = w_ref[...]
    x_hi = x.astype(BF16)
    x_lo = (x - x_hi.astype(F32)).astype(BF16)
    w_hi = w.astype(BF16)
    w_lo = (w - w_hi.astype(F32)).astype(BF16)
    logits = _dot(x_hi, w_hi) + _dot(x_lo, w_hi) + _dot(x_hi, w_lo) + b_ref[...]
    tm = logits.shape[0]
    lane = lax.broadcasted_iota(I32, logits.shape, 1)
    work = logits
    vals, sels, hots = [], [], []
    for _ in range(TOP_K):
        m = jnp.max(work, axis=1, keepdims=True)
        sel = jnp.min(jnp.where(work == m, lane, N_EXPERTS), axis=1, keepdims=True)
        hot = lane == sel
        vals.append(m)
        sels.append(sel)
        hots.append(hot)
        work = jnp.where(hot, NEG_BIG, work)
    exps = [jnp.exp(vk - vals[0]) for vk in vals]
    denom = exps[0] + exps[1] + exps[2] + exps[3]
    cnt = jnp.zeros(logits.shape, F32)
    for hot in hots:
        cnt = cnt + hot.astype(F32)
    ri = lax.broadcasted_iota(I32, (tm, tm), 0)
    rj = lax.broadcasted_iota(I32, (tm, tm), 1)
    tri = jnp.where(ri > rj, 1.0, 0.0).astype(BF16)
    pexcl = _dot(tri, cnt.astype(BF16)) + run_ref[...]
    lane_o = lax.broadcasted_iota(I32, (tm, LANES), 1)
    idx_o = jnp.zeros((tm, LANES), I32)
    prob_o = jnp.zeros((tm, LANES), F32)
    pre_o = jnp.zeros((tm, LANES), I32)
    for kk in range(TOP_K):
        pre_k = jnp.sum(jnp.where(hots[kk], pexcl, 0.0), axis=1, keepdims=True).astype(I32)
        idx_o = jnp.where(lane_o == kk, sels[kk], idx_o)
        prob_o = jnp.where(lane_o == kk, exps[kk] / denom, prob_o)
        pre_o = jnp.where(lane_o == kk, pre_k, pre_o)
    idx_ref[...] = idx_o
    prob_ref[...] = prob_o
    pre_ref[...] = pre_o
    run_ref[...] = run_ref[...] + jnp.sum(cnt, axis=0, keepdims=True)
    cnt_ref[...] = run_ref[...]


def _router(x, w, b, tm=256):
    t, dm = x.shape
    tm = min(tm, t)
    return pl.pallas_call(
        _router_kernel,
        grid=(t // tm,),
        in_specs=[pl.BlockSpec((tm, dm), lambda i: (i, 0)),
                  pl.BlockSpec((dm, N_EXPERTS), lambda i: (0, 0)),
                  pl.BlockSpec((1, N_EXPERTS), lambda i: (0, 0))],
        out_specs=[pl.BlockSpec((tm, LANES), lambda i: (i, 0)),
                   pl.BlockSpec((tm, LANES), lambda i: (i, 0)),
                   pl.BlockSpec((tm, LANES), lambda i: (i, 0)),
                   pl.BlockSpec((1, N_EXPERTS), lambda i: (0, 0))],
        out_shape=[jax.ShapeDtypeStruct((t, LANES), I32),
                   jax.ShapeDtypeStruct((t, LANES), F32),
                   jax.ShapeDtypeStruct((t, LANES), I32),
                   jax.ShapeDtypeStruct((1, N_EXPERTS), F32)],
        scratch_shapes=[pltpu.VMEM((1, N_EXPERTS), F32)],
        compiler_params=_cparams(("arbitrary",)),
        name="moe_router",
    )(x, w, b.reshape(1, N_EXPERTS))


def _row_copies(n, make_copy, wait):
    def body(jj, carry):
        for u in range(DMA_UNROLL):
            cp = make_copy(jj * DMA_UNROLL + u, jj * (DMA_UNROLL // TOP_K) + u // TOP_K, u % TOP_K)
            if wait:
                cp.wait()
            else:
                cp.start()
        return carry

    lax.fori_loop(0, n // DMA_UNROLL, body, 0)


def _dispatch_kernel(x_ref, dest_ref, xs_in_ref, xs_ref, sem):
    del xs_in_ref

    def row_copy(j, token, choice):
        del choice
        return pltpu.make_async_copy(x_ref.at[pl.ds(token, 1)], xs_ref.at[pl.ds(dest_ref[j], 1)], sem)

    _row_copies(dest_ref.shape[0], row_copy, wait=False)
    _row_copies(dest_ref.shape[0], row_copy, wait=True)


def _dispatch(x, dest_flat, xs_prev):
    t, dm = x.shape
    tb = min(MOE_TB, t)
    return pl.pallas_call(
        _dispatch_kernel,
        grid=(t // tb,),
        in_specs=[pl.BlockSpec((tb, dm), lambda i: (i, 0)),
                  pl.BlockSpec((tb * TOP_K,), lambda i: (i,), memory_space=pltpu.SMEM),
                  pl.BlockSpec(memory_space=pl.ANY)],
        out_specs=pl.BlockSpec(memory_space=pl.ANY),
        out_shape=jax.ShapeDtypeStruct(xs_prev.shape, xs_prev.dtype),
        scratch_shapes=[pltpu.SemaphoreType.DMA(())],
        input_output_aliases={2: 0},
        compiler_params=_cparams(("arbitrary",)),
        name="moe_dispatch",
    )(x, dest_flat, xs_prev)


def _expert_kernel(bexp_ref, nvalid_ref, x_ref, wg_ref, wu_ref, wd_ref, bg_ref, bu_ref, bd_ref, y_ref,
                   wg_s, wu_s, wd_s):
    i = pl.program_id(0)
    valid = i < nvalid_ref[0]
    changed = jnp.logical_or(i == 0, bexp_ref[i] != bexp_ref[jnp.maximum(i - 1, 0)])

    @pl.when(jnp.logical_and(valid, changed))
    def _():
        wg_s[...] = wg_ref[...].astype(BF16)
        wu_s[...] = wu_ref[...].astype(BF16)
        wd_s[...] = wd_ref[...].astype(BF16)

    @pl.when(valid)
    def _():
        half = x_ref.shape[1]
        x_lo, x_hi = _unpack_bf16_pair(x_ref[...])
        x_lo, x_hi = x_lo.astype(BF16), x_hi.astype(BF16)
        glin = _dot(x_lo, wg_s[:half, :]) + _dot(x_hi, wg_s[half:, :]) + bg_ref[...]
        ulin = _dot(x_lo, wu_s[:half, :]) + _dot(x_hi, wu_s[half:, :]) + bu_ref[...]
        glin = jnp.minimum(glin, SWIGLU_LIMIT)
        ulin = jnp.clip(ulin, -SWIGLU_LIMIT, SWIGLU_LIMIT)
        act = ((ulin + 1.0) * glin * _sigmoid(SWIGLU_ALPHA * glin)).astype(BF16)
        y_lo = _dot(act, wd_s[:, :half]) + bd_ref[:, :half]
        y_hi = _dot(act, wd_s[:, half:]) + bd_ref[:, half:]
        y_ref[...] = _pack_bf16_pair(y_lo, y_hi)

    @pl.when(jnp.logical_not(valid))
    def _():
        y_ref[...] = jnp.zeros_like(y_ref)


def _experts(xs, bexp, nvalid, layer, w_gate, b_gate, w_up, b_up, w_down, b_down):
    nrows, half = xs.shape
    dm = 2 * half
    tm = MOE_TM
    nb = nrows // tm
    ff = w_gate.shape[-1]

    def row_map(i, be, nv):
        return (jnp.maximum(jnp.minimum(i, nv[0] - 1), 0), 0)

    def w_map(i, be, nv):
        return (layer, be[i], 0, 0)

    grid_spec = pltpu.PrefetchScalarGridSpec(
        num_scalar_prefetch=2,
        grid=(nb,),
        in_specs=[pl.BlockSpec((tm, half), row_map),
                  pl.BlockSpec((None, None, dm, ff), w_map),
                  pl.BlockSpec((None, None, dm, ff), w_map),
                  pl.BlockSpec((None, None, ff, dm), w_map),
                  pl.BlockSpec((None, None, 1, ff), w_map),
                  pl.BlockSpec((None, None, 1, ff), w_map),
                  pl.BlockSpec((None, None, 1, dm), w_map)],
        out_specs=pl.BlockSpec((tm, half), lambda i, be, nv: (i, 0)),
        scratch_shapes=[pltpu.VMEM((dm, ff), BF16), pltpu.VMEM((dm, ff), BF16),
                        pltpu.VMEM((ff, dm), BF16)])
    nl = w_gate.shape[0]
    return pl.pallas_call(
        _expert_kernel,
        grid_spec=grid_spec,
        out_shape=jax.ShapeDtypeStruct((nrows, half), U32),
        compiler_params=_cparams(("arbitrary",)),
        name="moe_experts",
    )(bexp, nvalid, xs, w_gate, w_up, w_down, b_gate.reshape(nl, N_EXPERTS, 1, ff),
      b_up.reshape(nl, N_EXPERTS, 1, ff), b_down.reshape(nl, N_EXPERTS, 1, dm))


def _combine_kernel(h_ref, prob_ref, dest_ref, dnext_ref, ys_ref, g_ref, b_ref, o_ref, obf_ref, ybuf, sems):
    i = pl.program_id(0)
    nsteps = pl.num_programs(0)
    cur = lax.rem(i, 2)
    n = dest_ref.shape[0]

    def gather(dref, buf):
        def row_copy(j, token, choice):
            return pltpu.make_async_copy(ys_ref.at[pl.ds(dref[j], 1)],
                                         ybuf.at[buf, choice, pl.ds(token, 1)], sems.at[buf])
        return row_copy

    @pl.when(i == 0)
    def _():
        _row_copies(n, gather(dest_ref, cur), wait=False)

    @pl.when(i + 1 < nsteps)
    def _():
        _row_copies(n, gather(dnext_ref, 1 - cur), wait=False)

    _row_copies(n, gather(dest_ref, cur), wait=True)
    half = ybuf.shape[-1]
    prob = prob_ref[...]
    ff_lo, ff_hi = None, None
    for kk in range(TOP_K):
        lo, hi = _unpack_bf16_pair(ybuf[cur, kk])
        pk = prob[:, kk:kk + 1]
        ff_lo = pk * lo if ff_lo is None else ff_lo + pk * lo
        ff_hi = pk * hi if ff_hi is None else ff_hi + pk * hi
    z_lo = DEEPNORM_ALPHA * h_ref[:, :half] + ff_lo
    z_hi = DEEPNORM_ALPHA * h_ref[:, half:] + ff_hi
    inv_n = 1.0 / (2 * half)
    mu = (jnp.sum(z_lo, axis=-1, keepdims=True) + jnp.sum(z_hi, axis=-1, keepdims=True)) * inv_n
    z_lo = z_lo - mu
    z_hi = z_hi - mu
    var = (jnp.sum(z_lo * z_lo, axis=-1, keepdims=True)
           + jnp.sum(z_hi * z_hi, axis=-1, keepdims=True)) * inv_n
    rs = lax.rsqrt(var + LN_EPS)
    y_lo = z_lo * rs * g_ref[:, :half] + b_ref[:, :half]
    y_hi = z_hi * rs * g_ref[:, half:] + b_ref[:, half:]
    o_ref[:, :half] = y_lo
    o_ref[:, half:] = y_hi
    obf_ref[:, :half] = y_lo.astype(BF16)
    obf_ref[:, half:] = y_hi.astype(BF16)


def _combine(h, prob, dest_flat, ys, g, b):
    t, dm = h.shape
    tb = min(MOE_TB, t)
    nsteps = t // tb
    return pl.pallas_call(
        _combine_kernel,
        grid=(nsteps,),
        in_specs=[pl.BlockSpec((tb, dm), lambda i: (i, 0)),
                  pl.BlockSpec((tb, LANES), lambda i: (i, 0)),
                  pl.BlockSpec((tb * TOP_K,), lambda i: (i,), memory_space=pltpu.SMEM),
                  pl.BlockSpec((tb * TOP_K,), lambda i: (jnp.minimum(i + 1, nsteps - 1),),
                               memory_space=pltpu.SMEM),
                  pl.BlockSpec(memory_space=pl.ANY),
                  pl.BlockSpec((1, dm), lambda i: (0, 0)),
                  pl.BlockSpec((1, dm), lambda i: (0, 0))],
        out_specs=[pl.BlockSpec((tb, dm), lambda i: (i, 0)),
                   pl.BlockSpec((tb, dm), lambda i: (i, 0))],
        out_shape=[jax.ShapeDtypeStruct((t, dm), F32), jax.ShapeDtypeStruct((t, dm), BF16)],
        scratch_shapes=[pltpu.VMEM((2, TOP_K, tb, dm // 2), U32), pltpu.SemaphoreType.DMA((2,))],
        compiler_params=_cparams(("arbitrary",)),
        name="moe_combine",
    )(h, prob, dest_flat, dest_flat, ys, g.reshape(1, dm), b.reshape(1, dm))


def _moe_rows(t):
    return ((t * TOP_K) // MOE_TM + N_EXPERTS) * MOE_TM


def _moe_layer(h, hp, xs_prev, layer, w_router, b_router, w_gate, b_gate, w_up, b_up, w_down, b_down,
               ln_g, ln_b):
    t, dm = h.shape
    tm = MOE_TM
    idx, prob, pre, counts = _router(h, w_router, b_router)
    counts = counts.reshape(N_EXPERTS).astype(I32)
    nblk = (counts + tm - 1) // tm
    blk_end = jnp.cumsum(nblk)
    gstart = ((blk_end - nblk) * tm).astype(I32)
    nvalid = blk_end[-1:].astype(I32)
    nb = _moe_rows(t) // tm
    blk = jnp.minimum(jnp.arange(nb, dtype=I32), nvalid[0] - 1)
    bexp = jnp.sum((blk[:, None] >= blk_end[None, :]).astype(I32), axis=1).astype(I32)
    idx4, pre4 = idx[:, :TOP_K], pre[:, :TOP_K]
    hot = idx4[:, :, None] == jnp.arange(N_EXPERTS, dtype=I32)[None, None, :]
    dest = pre4 + jnp.sum(jnp.where(hot, gstart[None, None, :], 0), axis=-1)
    dest_flat = dest.reshape(t * TOP_K).astype(I32)
    xs = _dispatch(hp, dest_flat, xs_prev)
    ys = _experts(xs, bexp, nvalid, layer, w_gate, b_gate, w_up, b_up, w_down, b_down)
    hn, hnb = _combine(h, prob, dest_flat, ys, ln_g, ln_b)
    return hn, hnb, xs


def _even_layer(h, hb, w_in, w_out, s5_b_re, s5_b_im, s5_c_re, s5_c_im, s5_d, s5_a_re, s5_a_im,
                s5_log_dt, s5_w_glu, s5_b_glu, ln_g, ln_b):
    t = h.shape[0]
    w_in_p = jnp.concatenate([w_in[:, :EVEN_QKVG], _s5_channel_major(w_in[:, EVEN_QKVG:], 1)], axis=1)
    w_glu_p = _s5_channel_major(_s5_channel_major(s5_w_glu, 0), 1)
    b_glu_p = _s5_channel_major(s5_b_glu, 0)
    w_out_p = jnp.concatenate([w_out[:RET_WIDTH], _s5_channel_major(w_out[RET_WIDTH:], 0)], axis=0)
    p = _matmul(hb, w_in_p.astype(BF16), w_in.shape[1], BF16, 1024, 512)
    o_ret = _retention(p, t)
    y = _s5(p[:, EVEN_QKVG:], s5_b_re, s5_b_im, s5_c_re, s5_c_im, s5_d, s5_a_re, s5_a_im, s5_log_dt)
    y = _matmul_glu(y, w_glu_p.astype(BF16), b_glu_p)
    mixed = jnp.concatenate([o_ret, y], axis=-1)
    return _matmul_ln(mixed, w_out_p.astype(BF16), h, ln_g, ln_b)


def _odd_layer(h, hb, w_in, w_out, conv_w, a_log, dt_bias, norm_w, ln_g, ln_b):
    t = h.shape[0]
    w16 = w_in.astype(BF16)
    p = _matmul(hb, w16, GDN_MAIN, BF16, 1024, 512)
    ab = _matmul(hb, w_in[:, GDN_MAIN:].astype(BF16), 2 * GDN_HEADS, F32, 1024, 2 * GDN_HEADS)
    o = _gated_deltanet(p, ab, conv_w, a_log, dt_bias, norm_w, t)
    return _matmul_ln(o, w_out.astype(BF16), h, ln_g, ln_b)


def kernel(x, ev_w_in, ev_w_out, ev_s5_b_re, ev_s5_b_im, ev_s5_c_re, ev_s5_c_im, ev_s5_d, ev_s5_a_re,
           ev_s5_a_im, ev_s5_log_dt, ev_s5_w_glu, ev_s5_b_glu, ev_ln_g, ev_ln_b, od_w_in, od_w_out,
           od_conv_w, od_a_log, od_dt_bias, od_norm_w, od_ln_g, od_ln_b, moe_w_router, moe_b_router,
           moe_w_gate, moe_b_gate, moe_w_up, moe_b_up, moe_w_down, moe_b_down, moe_ln_g, moe_ln_b):
    bsz, seq, dm = x.shape
    outs = []
    xs = jnp.zeros((_moe_rows(seq), dm // 2), U32)
    for bi in range(bsz):
        h = x[bi]
        hb = h.astype(BF16)
        for layer in range(DEPTH):
            p = layer // 2
            if layer % 2 == 0:
                h, hp = _even_layer(h, hb, ev_w_in[p], ev_w_out[p], ev_s5_b_re[p], ev_s5_b_im[p],
                                    ev_s5_c_re[p], ev_s5_c_im[p], ev_s5_d[p], ev_s5_a_re[p],
                                    ev_s5_a_im[p], ev_s5_log_dt[p], ev_s5_w_glu[p], ev_s5_b_glu[p],
                                    ev_ln_g[p], ev_ln_b[p])
            else:
                h, hp = _odd_layer(h, hb, od_w_in[p], od_w_out[p], od_conv_w[p], od_a_log[p],
                                   od_dt_bias[p], od_norm_w[p], od_ln_g[p], od_ln_b[p])
            h, hb, xs = _moe_layer(h, hp, xs, layer, moe_w_router[layer], moe_b_router[layer],
                                   moe_w_gate, moe_b_gate, moe_w_up, moe_b_up, moe_w_down, moe_b_down,
                                   moe_ln_g[layer], moe_ln_b[layer])
        outs.append(h)
    return jnp.stack(outs, axis=0)
```

```python
import functools
import math

import jax
import jax.numpy as jnp
from jax import lax
from jax.experimental import pallas as pl
from jax.experimental.pallas import tpu as pltpu

F32 = jnp.float32
BF16 = jnp.bfloat16
I32 = jnp.int32
U32 = jnp.uint32
HIGHEST = lax.Precision.HIGHEST

D_MODEL = 4096
DEPTH = 4
CHUNK = 64
RET_HEADS = 8
RET_WIDTH = D_MODEL // 2
RET_DV = RET_WIDTH // RET_HEADS
RET_DK = RET_DV // 2
ROPE_BASE = 10000.0
S5_WIDTH = D_MODEL - RET_WIDTH
S5_GROUP = 16
S5_GROUPS = S5_WIDTH // S5_GROUP
S5_STATE = 64
EVEN_QKVG = 2 * RET_HEADS * RET_DK + 2 * RET_WIDTH
GDN_HEADS = 16
GDN_DV = D_MODEL // GDN_HEADS
GDN_DK = GDN_DV // 2
CONV_K = 4
GDN_MAIN = 2 * GDN_HEADS * GDN_DK + 2 * GDN_HEADS * GDN_DV
N_EXPERTS = 32
TOP_K = 4
EXPERT_FF = D_MODEL // 16
SWIGLU_LIMIT = 7.0
SWIGLU_ALPHA = 1.702
DEEPNORM_ALPHA = (2 * DEPTH) ** 0.25
LN_EPS = 1e-5
NORM_EPS = 1e-6

V7X_VMEM_LIMIT_BYTES = 56 * 1024 * 1024
LANES = 128
BF16_SUBLANES = 16
S5_SUB = 16
RET_BLOCK = 256
GDN_BLOCK = 256
GDN_SUB = 256
GDN_PREP_HEADS = 4
MOE_TM = 512
MOE_TB = 128
DMA_UNROLL = 8
NEG_BIG = -3.0e38


def _cparams(sem, vmem=V7X_VMEM_LIMIT_BYTES):
    return pltpu.CompilerParams(dimension_semantics=sem, vmem_limit_bytes=vmem)


def _dot(a, b):
    return jnp.dot(a, b, preferred_element_type=F32)


def _dot_nt(a, b):
    return lax.dot_general(a, b, (((1,), (1,)), ((), ())), preferred_element_type=F32)


def _dot_tn(a, b):
    return lax.dot_general(a, b, (((0,), (0,)), ((), ())), preferred_element_type=F32)


def _dot_hi(a, b):
    return jnp.dot(a, b, preferred_element_type=F32, precision=HIGHEST)


def _sigmoid(x):
    return 1.0 / (1.0 + jnp.exp(-x))


def _mm_kernel(x_ref, w_ref, o_ref):
    o_ref[...] = _dot(x_ref[...].astype(BF16), w_ref[...]).astype(o_ref.dtype)


def _matmul(x, w, n, out_dtype, tm, tn):
    m, k = x.shape
    tm, tn = min(tm, m), min(tn, n)
    return pl.pallas_call(
        _mm_kernel,
        grid=(m // tm, n // tn),
        in_specs=[pl.BlockSpec((tm, k), lambda i, j: (i, 0)),
                  pl.BlockSpec((k, tn), lambda i, j: (0, j))],
        out_specs=pl.BlockSpec((tm, tn), lambda i, j: (i, j)),
        out_shape=jax.ShapeDtypeStruct((m, n), out_dtype),
        compiler_params=_cparams(("parallel", "parallel")),
        name="proj_matmul",
    )(x, w)


def _pack_bf16_pair(lo, hi):
    lo_bits = lax.bitcast_convert_type(lo.astype(BF16).astype(F32), U32)
    hi_bits = lax.bitcast_convert_type(hi.astype(BF16).astype(F32), U32)
    return jnp.right_shift(lo_bits, jnp.uint32(16)) | (hi_bits & jnp.uint32(0xFFFF0000))


def _unpack_bf16_pair(word):
    lo = lax.bitcast_convert_type(jnp.left_shift(word, jnp.uint32(16)), F32)
    hi = lax.bitcast_convert_type(word & jnp.uint32(0xFFFF0000), F32)
    return lo, hi


def _mm_ln_kernel(x_ref, w_ref, res_ref, g_ref, b_ref, o_ref, hp_ref, z_ref, *, nj):
    j = pl.program_id(1)
    z_ref[j] = DEEPNORM_ALPHA * res_ref[...] + _dot(x_ref[...].astype(BF16), w_ref[...])

    @pl.when(j == nj - 1)
    def _():
        tn = z_ref.shape[2]
        n = nj * tn
        tot = z_ref[0]
        for jj in range(1, nj):
            tot = tot + z_ref[jj]
        mu = jnp.sum(tot, axis=-1, keepdims=True) * (1.0 / n)
        sq = jnp.square(z_ref[0] - mu)
        for jj in range(1, nj):
            sq = sq + jnp.square(z_ref[jj] - mu)
        rs = lax.rsqrt(jnp.sum(sq, axis=-1, keepdims=True) * (1.0 / n) + LN_EPS)
        half = nj // 2
        for jj in range(half):
            ca = slice(jj * tn, (jj + 1) * tn)
            cb = slice((jj + half) * tn, (jj + half + 1) * tn)
            ya = (z_ref[jj] - mu) * rs * g_ref[:, ca] + b_ref[:, ca]
            yb = (z_ref[jj + half] - mu) * rs * g_ref[:, cb] + b_ref[:, cb]
            o_ref[:, ca] = ya
            o_ref[:, cb] = yb
            hp_ref[:, ca] = _pack_bf16_pair(ya, yb)


def _matmul_ln(x, w, res, g, b, tm=512, tn=512):
    m, kdim = x.shape
    n = w.shape[1]
    tm, tn = min(tm, m), min(tn, n)
    nj = n // tn
    return pl.pallas_call(
        functools.partial(_mm_ln_kernel, nj=nj),
        grid=(m // tm, nj),
        in_specs=[pl.BlockSpec((tm, kdim), lambda i, j: (i, 0)),
                  pl.BlockSpec((kdim, tn), lambda i, j: (0, j)),
                  pl.BlockSpec((tm, tn), lambda i, j: (i, j)),
                  pl.BlockSpec((1, n), lambda i, j: (0, 0)),
                  pl.BlockSpec((1, n), lambda i, j: (0, 0))],
        out_specs=[pl.BlockSpec((tm, n), lambda i, j: (i, 0)),
                   pl.BlockSpec((tm, n // 2), lambda i, j: (i, 0))],
        out_shape=[jax.ShapeDtypeStruct((m, n), F32), jax.ShapeDtypeStruct((m, n // 2), U32)],
        scratch_shapes=[pltpu.VMEM((nj, tm, tn), F32)],
        compiler_params=_cparams(("parallel", "arbitrary")),
        name="proj_layernorm",
    )(x, w, res, g.reshape(1, n), b.reshape(1, n))


def _mm_glu_kernel(x_ref, yt_ref, w_ref, b_ref, o_ref):
    z = _dot(x_ref[...], w_ref[...]) + b_ref[...]
    o_ref[...] = (yt_ref[...].astype(F32) * _sigmoid(z)).astype(o_ref.dtype)


def _matmul_glu(y, w, b, tm=1024, tn=512):
    m, k = y.shape
    n = w.shape[1]
    tm, tn = min(tm, m), min(tn, n)
    return pl.pallas_call(
        _mm_glu_kernel,
        grid=(m // tm, n // tn),
        in_specs=[pl.BlockSpec((tm, k), lambda i, j: (i, 0)),
                  pl.BlockSpec((tm, tn), lambda i, j: (i, j)),
                  pl.BlockSpec((k, tn), lambda i, j: (0, j)),
                  pl.BlockSpec((1, tn), lambda i, j: (0, j))],
        out_specs=pl.BlockSpec((tm, tn), lambda i, j: (i, j)),
        out_shape=jax.ShapeDtypeStruct((m, n), BF16),
        compiler_params=_cparams(("parallel", "parallel")),
        name="s5_glu",
    )(y, y, w, b.reshape(1, n))


def _ret_kernel(bdec_ref, q_ref, k_ref, v_ref, gate_ref, cos_ref, sin_ref, dmask_ref, qdec_ref,
                kdec_ref, o_ref, s_ref):
    h = pl.program_id(0)

    @pl.when(pl.program_id(1) == 0)
    def _():
        s_ref[...] = jnp.zeros_like(s_ref)

    cos = cos_ref[...]
    sin = sin_ref[...]
    q = q_ref[...].astype(F32)
    k = k_ref[...].astype(F32)
    half = RET_DK // 2
    qr = q * cos + pltpu.roll(q, half, 1) * sin
    kr = (k * cos + pltpu.roll(k, half, 1) * sin) * (RET_DK ** -0.5)
    v = v_ref[...].astype(BF16)
    scores = _dot_nt(qr.astype(BF16), kr.astype(BF16)) * dmask_ref[0]
    o = _dot(scores.astype(BF16), v)
    state = s_ref[...]
    o = o + _dot((qr * qdec_ref[0]).astype(BF16), state.astype(BF16))
    s_ref[...] = state * bdec_ref[h] + _dot_tn((kr * kdec_ref[0]).astype(BF16), v)
    mu = jnp.mean(o, axis=-1, keepdims=True)
    oc = o - mu
    var = jnp.mean(oc * oc, axis=-1, keepdims=True)
    on = oc * lax.rsqrt(var + NORM_EPS)
    gate = gate_ref[...].astype(F32)
    o_ref[...] = (on * (gate * _sigmoid(gate))).astype(o_ref.dtype)


def _retention_tables(t, bt):
    half = RET_DK // 2
    inv_freq = jnp.power(ROPE_BASE, -jnp.arange(half, dtype=F32) / half)
    ang = jnp.arange(t, dtype=F32)[:, None] * inv_freq[None, :]
    cos, sin = jnp.cos(ang), jnp.sin(ang)
    cos2 = jnp.concatenate([cos, cos], axis=-1)
    sin2 = jnp.concatenate([-sin, sin], axis=-1)
    log_gamma = jnp.log1p(-jnp.exp2(-5.0 - jnp.arange(RET_HEADS, dtype=F32)))
    pos = jnp.arange(bt, dtype=F32)
    dist = jnp.abs(pos[:, None] - pos[None, :])
    chunk_id = jnp.arange(bt) // CHUNK
    visible = chunk_id[None, :] <= chunk_id[:, None]
    dmask = jnp.where(visible[None], jnp.exp(log_gamma[:, None, None] * dist[None]), 0.0)
    qdec = jnp.exp(log_gamma[:, None] * (pos + 1.0))
    kdec = jnp.exp(log_gamma[:, None] * (bt - 1.0 - pos))
    qdec = jnp.broadcast_to(qdec[:, :, None], (RET_HEADS, bt, RET_DK))
    kdec = jnp.broadcast_to(kdec[:, :, None], (RET_HEADS, bt, RET_DK))
    bdec = jnp.exp(log_gamma * bt)
    return cos2, sin2, dmask, qdec, kdec, bdec


def _retention(p, t):
    bt = min(RET_BLOCK, t)
    cos2, sin2, dmask, qdec, kdec, bdec = _retention_tables(t, bt)
    kq = RET_HEADS
    kv = 2 * RET_HEADS * RET_DK // RET_DV
    kg = kv + RET_HEADS
    grid_spec = pltpu.PrefetchScalarGridSpec(
        num_scalar_prefetch=0,
        grid=(RET_HEADS, t // bt),
        in_specs=[pl.BlockSpec(memory_space=pltpu.SMEM),
                  pl.BlockSpec((bt, RET_DK), lambda h, i: (i, h)),
                  pl.BlockSpec((bt, RET_DK), lambda h, i: (i, kq + h)),
                  pl.BlockSpec((bt, RET_DV), lambda h, i: (i, kv + h)),
                  pl.BlockSpec((bt, RET_DV), lambda h, i: (i, kg + h)),
                  pl.BlockSpec((bt, RET_DK), lambda h, i: (i, 0)),
                  pl.BlockSpec((bt, RET_DK), lambda h, i: (i, 0)),
                  pl.BlockSpec((1, bt, bt), lambda h, i: (h, 0, 0)),
                  pl.BlockSpec((1, bt, RET_DK), lambda h, i: (h, 0, 0)),
                  pl.BlockSpec((1, bt, RET_DK), lambda h, i: (h, 0, 0))],
        out_specs=pl.BlockSpec((bt, RET_DV), lambda h, i: (i, h)),
        scratch_shapes=[pltpu.VMEM((RET_DK, RET_DV), F32)])
    return pl.pallas_call(
        _ret_kernel,
        grid_spec=grid_spec,
        out_shape=jax.ShapeDtypeStruct((t, RET_WIDTH), BF16),
        compiler_params=_cparams(("parallel", "arbitrary")),
        name="retention",
    )(bdec, p, p, p, p, cos2, sin2, dmask, qdec, kdec)


def _gelu_tanh(x):
    c = math.sqrt(2.0 / math.pi)
    return 0.5 * x * (1.0 + jnp.tanh(c * (x + 0.044715 * (x * x * x))))


def _s5_kernel(u_ref, m_ref, bm_ref, cm_ref, pw_ref, d_ref, y_ref, *, nlev):
    u = u_ref[0].astype(F32)
    ub = u.astype(BF16)
    nc = u.shape[0]
    hstate = _dot(ub, bm_ref[0])
    row = lax.broadcasted_iota(I32, hstate.shape, 0)
    pw = pw_ref[0]
    for lev in range(nlev):
        s = 1 << lev
        hs = jnp.where(row >= s, pltpu.roll(hstate, s, 0), 0.0)
        hstate = (hstate + pw[2 * lev:2 * lev + 1, :] * hs
                  + pw[2 * lev + 1:2 * lev + 2, :] * pltpu.roll(hs, S5_STATE, 1))
    if nc > 1:
        hprev = jnp.where(row >= 1, pltpu.roll(hstate, 1, 0), 0.0)
    else:
        hprev = jnp.zeros_like(hstate)
    y = _dot(ub, m_ref[0]) + _dot(hprev.astype(BF16), cm_ref[0]) + d_ref[0] * u
    y_ref[0] = _gelu_tanh(y).astype(y_ref.dtype)


def _s5_tables(b_re, b_im, c_re, c_im, d, a_re, a_im, log_dt, nc):
    g, p, i = S5_GROUPS, S5_STATE, S5_GROUP
    l = S5_SUB
    dt = jnp.exp(log_dt)[:, None]
    lam_re = jnp.minimum(a_re, -1e-4)
    lam_im = a_im
    mag = jnp.exp(lam_re * dt)
    ab_re = mag * jnp.cos(lam_im * dt)
    ab_im = mag * jnp.sin(lam_im * dt)
    den = jnp.square(lam_re) + jnp.square(lam_im)
    f_re = ((ab_re - 1.0) * lam_re + ab_im * lam_im) / den
    f_im = (ab_im * lam_re - (ab_re - 1.0) * lam_im) / den
    bb_re = f_re[..., None] * b_re - f_im[..., None] * b_im
    bb_im = f_re[..., None] * b_im + f_im[..., None] * b_re

    def apow(n):
        n = jnp.asarray(n, F32)[..., None, None]
        m = jnp.exp(n * (lam_re * dt))
        return m * jnp.cos(n * (lam_im * dt)), m * jnp.sin(n * (lam_im * dt))

    wr, wi = apow(jnp.arange(l))
    xr = wr[..., None] * bb_re[None] - wi[..., None] * bb_im[None]
    xi = wr[..., None] * bb_im[None] + wi[..., None] * bb_re[None]
    kern = (jnp.einsum('gip,jgpk->gjik', c_re, xr, precision=HIGHEST)
            - jnp.einsum('gip,jgpk->gjik', c_im, xi, precision=HIGHEST))
    lag = jnp.arange(l)[None, :] - jnp.arange(l)[:, None]
    toe = kern[:, jnp.clip(lag, 0, l - 1)]
    toe = jnp.where((lag >= 0)[None, :, :, None, None], toe, 0.0)
    m_mat = toe.transpose(0, 1, 4, 2, 3).reshape(g, l * i, l * i)
    wr, wi = apow(l - 1 - jnp.arange(l))
    br = wr[..., None] * bb_re[None] - wi[..., None] * bb_im[None]
    bi = wr[..., None] * bb_im[None] + wi[..., None] * bb_re[None]
    bm = jnp.concatenate([br, bi], axis=2)
    bm = bm.transpose(1, 0, 3, 2).reshape(g, l * i, 2 * p)
    wr, wi = apow(jnp.arange(l) + 1.0)
    cr = (c_re.transpose(0, 2, 1)[None] * wr[..., None]
          - c_im.transpose(0, 2, 1)[None] * wi[..., None])
    ci = (-c_re.transpose(0, 2, 1)[None] * wi[..., None]
          - c_im.transpose(0, 2, 1)[None] * wr[..., None])
    cm = jnp.concatenate([cr, ci], axis=2)
    cm = cm.transpose(1, 2, 0, 3).reshape(g, 2 * p, l * i)
    nlev = max(1, int(math.ceil(math.log2(nc)))) if nc > 1 else 0
    if nlev:
        wr, wi = apow(float(l) * (2.0 ** jnp.arange(nlev)))
        pw = jnp.stack([jnp.concatenate([wr, wr], axis=-1),
                        jnp.concatenate([-wi, wi], axis=-1)], axis=1)
        pw = pw.reshape(2 * nlev, g, 2 * p).transpose(1, 0, 2)
    else:
        pw = jnp.zeros((g, 2, 2 * p), F32)
    dd = jnp.tile(d.reshape(g, 1, i), (1, l, 1)).reshape(g, 1, l * i)
    return m_mat.astype(BF16), bm.astype(BF16), cm.astype(BF16), pw, dd, nlev


def _s5_channel_major(w, axis):
    shape = w.shape
    w = w.reshape(shape[:axis] + (S5_GROUPS, S5_GROUP) + shape[axis + 1:])
    return jnp.swapaxes(w, axis, axis + 1).reshape(shape)


def _s5(u, b_re, b_im, c_re, c_im, d, a_re, a_im, log_dt):
    t = u.shape[0]
    g, l, i = S5_GROUPS, S5_SUB, S5_GROUP
    nc = t // l
    m_mat, bm, cm, pw, dd, nlev = _s5_tables(b_re, b_im, c_re, c_im, d, a_re, a_im, log_dt, nc)
    ug = u.reshape(t * i, g).T.reshape(g, nc, l * i)
    npw = pw.shape[1]
    y = pl.pallas_call(
        functools.partial(_s5_kernel, nlev=nlev),
        grid=(g,),
        in_specs=[pl.BlockSpec((1, nc, l * i), lambda j: (j, 0, 0)),
                  pl.BlockSpec((1, l * i, l * i), lambda j: (j, 0, 0)),
                  pl.BlockSpec((1, l * i, 2 * S5_STATE), lambda j: (j, 0, 0)),
                  pl.BlockSpec((1, 2 * S5_STATE, l * i), lambda j: (j, 0, 0)),
                  pl.BlockSpec((1, npw, 2 * S5_STATE), lambda j: (j, 0, 0)),
                  pl.BlockSpec((1, 1, l * i), lambda j: (j, 0, 0))],
        out_specs=pl.BlockSpec((1, nc, l * i), lambda j: (j, 0, 0)),
        out_shape=jax.ShapeDtypeStruct((g, nc, l * i), BF16),
        compiler_params=_cparams(("parallel",)),
        name="s5_scan",
    )(ug, m_mat, bm, cm, pw, dd)
    return y.reshape(g, t * i).T.reshape(t, g * i)


def _causal_conv_silu(x, halo, w):
    row = lax.broadcasted_iota(I32, x.shape, 0)
    y = w[CONV_K - 1:CONV_K, :] * x
    for j in range(1, CONV_K):
        head = jnp.concatenate([pltpu.roll(halo, j, 0)] * (x.shape[0] // halo.shape[0]), axis=0)
        shifted = jnp.where(row >= j, pltpu.roll(x, j, 0), head)
        y = y + w[CONV_K - 1 - j:CONV_K - j, :] * shifted
    return y * _sigmoid(y)


def _softplus(x):
    return jnp.maximum(x, 0.0) + jnp.log(1.0 + jnp.exp(-jnp.abs(x)))


def _gdn_prep_kernel(nega_ref, dtb_ref, q_ref, k_ref, v_ref, qh_ref, kh_ref, vh_ref, wq_ref, wk_ref,
                     wv_ref, ab_ref, abt_ref, wvo_ref, kco_ref, qgo_ref, kdo_ref, ato_ref, ego_ref):
    hp = pl.program_id(0)
    keep = (pl.program_id(1) > 0).astype(F32)
    q_all = _causal_conv_silu(q_ref[...].astype(F32), qh_ref[...].astype(F32) * keep, wq_ref[...])
    k_all = _causal_conv_silu(k_ref[...].astype(F32), kh_ref[...].astype(F32) * keep, wk_ref[...])
    v_all = _causal_conv_silu(v_ref[...].astype(F32), vh_ref[...].astype(F32) * keep, wv_ref[...])
    bt = q_all.shape[0]
    ab = ab_ref[...]
    lane = lax.broadcasted_iota(I32, ab.shape, 1)
    abt = abt_ref[...]
    sub = lax.broadcasted_iota(I32, abt.shape, 0)

    sb = GDN_SUB
    bi = lax.broadcasted_iota(I32, (sb, sb), 0)
    bj = lax.broadcasted_iota(I32, (sb, sb), 1)
    same = (bi // CHUNK) == (bj // CHUNK)
    lower = jnp.logical_and(same, bi >= bj)
    strict = jnp.logical_and(same, bi > bj)
    upper = jnp.logical_and(same, bi <= bj)
    eye = (bi == bj).astype(F32)

    for hh in range(GDN_PREP_HEADS):
        h = hp * GDN_PREP_HEADS + hh
        q_h = q_all[:, hh * GDN_DK:(hh + 1) * GDN_DK]
        k_h = k_all[:, hh * GDN_DK:(hh + 1) * GDN_DK]
        v_h = v_all[:, hh * GDN_DV:(hh + 1) * GDN_DV]
        q_h = q_h * lax.rsqrt(jnp.sum(q_h * q_h, axis=-1, keepdims=True) + NORM_EPS) * (GDN_DK ** -0.5)
        k_h = k_h * lax.rsqrt(jnp.sum(k_h * k_h, axis=-1, keepdims=True) + NORM_EPS)
        neg_a = nega_ref[h]
        dtb = dtb_ref[h]
        a_col = jnp.sum(jnp.where(lane == h, ab, 0.0), axis=1, keepdims=True)
        b_col = jnp.sum(jnp.where(lane == GDN_HEADS + h, ab, 0.0), axis=1, keepdims=True)
        a_row = jnp.sum(jnp.where(sub == h, abt, 0.0), axis=0, keepdims=True)
        g_col_h = neg_a * _softplus(a_col + dtb)
        g_row_h = neg_a * _softplus(a_row + dtb)
        beta_h = _sigmoid(b_col)
        for si in range(bt // sb):
            rs = slice(si * sb, (si + 1) * sb)
            q, k, v = q_h[rs], k_h[rs], v_h[rs]
            g_col, g_row, beta = g_col_h[rs], g_row_h[:, rs], beta_h[rs]
            gcum_col = jnp.sum(jnp.where(lower, g_row, 0.0), axis=1, keepdims=True)
            gcum_row = jnp.sum(jnp.where(upper, g_col, 0.0), axis=0, keepdims=True)
            gtot_col = jnp.sum(jnp.where(same, g_row, 0.0), axis=1, keepdims=True)
            decay = jnp.where(lower, jnp.exp(jnp.where(lower, gcum_col - gcum_row, 0.0)), 0.0)
            k_beta = k * beta
            v_beta = v * beta
            k16 = k.astype(BF16)
            lmat = jnp.where(strict, _dot_nt(k_beta.astype(BF16), k16) * decay, 0.0)
            tinv = eye - lmat
            l16 = lmat.astype(BF16)
            pw = _dot(l16, l16)
            n = 2
            while True:
                tinv = tinv + _dot(tinv.astype(BF16), pw.astype(BF16))
                n *= 2
                if n >= CHUNK:
                    break
                p16 = pw.astype(BF16)
                pw = _dot(p16, p16)
            t16 = tinv.astype(BF16)
            eg = jnp.exp(gcum_col)
            w_v = _dot(t16, v_beta.astype(BF16))
            k_cum = _dot(t16, (k_beta * eg).astype(BF16))
            attn = jnp.where(lower, _dot_nt(q.astype(BF16), k16) * decay, 0.0)
            folded = attn[:, :LANES]
            for c in range(1, sb // LANES):
                folded = folded + attn[:, c * LANES:(c + 1) * LANES]
            folded = folded + pltpu.roll(folded, CHUNK, 1)
            wvo_ref[rs, hh * GDN_DV:(hh + 1) * GDN_DV] = w_v.astype(BF16)
            kco_ref[rs, hh * GDN_DK:(hh + 1) * GDN_DK] = k_cum.astype(BF16)
            qgo_ref[rs, hh * GDN_DK:(hh + 1) * GDN_DK] = (q * eg).astype(BF16)
            kdo_ref[rs, hh * GDN_DK:(hh + 1) * GDN_DK] = (k * jnp.exp(gtot_col - gcum_col)).astype(BF16)
            ato_ref[rs, hh * LANES:(hh + 1) * LANES] = folded.astype(BF16)
            ego_ref[rs, hh * LANES:(hh + 1) * LANES] = jnp.broadcast_to(jnp.exp(gtot_col), (sb, LANES))


def _gdn_scan_kernel(wv_ref, kc_ref, qg_ref, kd_ref, at_ref, eg_ref, gate_ref, nw_ref, o_ref, s_ref):
    @pl.when(pl.program_id(0) == 0)
    def _():
        s_ref[...] = jnp.zeros_like(s_ref)

    nw = nw_ref[...]

    def chunk_step(c, carry):
        r0 = pl.multiple_of(c * CHUNK, CHUNK)
        rows = pl.ds(r0, CHUNK)
        for h in range(GDN_HEADS):
            ck = slice(h * GDN_DK, (h + 1) * GDN_DK)
            cv = slice(h * GDN_DV, (h + 1) * GDN_DV)
            state = s_ref[h]
            s16 = state.astype(BF16)
            ks = _dot(jnp.concatenate([kc_ref[rows, ck], qg_ref[rows, ck]], axis=0), s16)
            v_new = wv_ref[rows, cv].astype(F32) - ks[:CHUNK]
            vn16 = v_new.astype(BF16)
            attn = at_ref[rows, h * LANES:h * LANES + CHUNK]
            o = ks[CHUNK:] + _dot(attn, vn16)
            eg = eg_ref[pl.ds(r0, 1), h * LANES:h * LANES + 1]
            s_ref[h] = state * eg + _dot_tn(kd_ref[rows, ck], vn16)
            rms = lax.rsqrt(jnp.mean(o * o, axis=-1, keepdims=True) + NORM_EPS)
            gc = gate_ref[rows, cv].astype(F32)
            o_ref[rows, cv] = (o * rms * nw * (gc * _sigmoid(gc))).astype(o_ref.dtype)
        return carry

    lax.fori_loop(0, wv_ref.shape[0] // CHUNK, chunk_step, 0)


def _gated_deltanet(p, ab, conv_w, a_log, dt_bias, norm_w, t):
    bt = min(GDN_BLOCK, t)
    hq, hg = GDN_HEADS, GDN_PREP_HEADS
    wk_, wv_ = hg * GDN_DK, hg * GDN_DV
    kk = hq * GDN_DK // wk_
    kv = 2 * hq * GDN_DK // wv_
    hb = bt // BF16_SUBLANES

    def halo_map(off):
        return lambda g, i: (jnp.maximum(i * hb - 1, 0), off + g)

    prep_spec = pltpu.PrefetchScalarGridSpec(
        num_scalar_prefetch=0,
        grid=(hq // hg, t // bt),
        in_specs=[pl.BlockSpec(memory_space=pltpu.SMEM),
                  pl.BlockSpec(memory_space=pltpu.SMEM),
                  pl.BlockSpec((bt, wk_), lambda g, i: (i, g)),
                  pl.BlockSpec((bt, wk_), lambda g, i: (i, kk + g)),
                  pl.BlockSpec((bt, wv_), lambda g, i: (i, kv + g)),
                  pl.BlockSpec((BF16_SUBLANES, wk_), halo_map(0)),
                  pl.BlockSpec((BF16_SUBLANES, wk_), halo_map(kk)),
                  pl.BlockSpec((BF16_SUBLANES, wv_), halo_map(kv)),
                  pl.BlockSpec((CONV_K, wk_), lambda g, i: (0, g)),
                  pl.BlockSpec((CONV_K, wk_), lambda g, i: (0, kk + g)),
                  pl.BlockSpec((CONV_K, wv_), lambda g, i: (0, kv + g)),
                  pl.BlockSpec((bt, 2 * hq), lambda g, i: (i, 0)),
                  pl.BlockSpec((2 * hq, bt), lambda g, i: (0, i))],
        out_specs=[pl.BlockSpec((bt, wv_), lambda g, i: (i, g)),
                   pl.BlockSpec((bt, wk_), lambda g, i: (i, g)),
                   pl.BlockSpec((bt, wk_), lambda g, i: (i, g)),
                   pl.BlockSpec((bt, wk_), lambda g, i: (i, g)),
                   pl.BlockSpec((bt, hg * LANES), lambda g, i: (i, g)),
                   pl.BlockSpec((bt, hg * LANES), lambda g, i: (i, g))])
    w_v, k_cum, q_g, k_d, attn, eg = pl.pallas_call(
        _gdn_prep_kernel,
        grid_spec=prep_spec,
        out_shape=[jax.ShapeDtypeStruct((t, hq * GDN_DV), BF16),
                   jax.ShapeDtypeStruct((t, hq * GDN_DK), BF16),
                   jax.ShapeDtypeStruct((t, hq * GDN_DK), BF16),
                   jax.ShapeDtypeStruct((t, hq * GDN_DK), BF16),
                   jax.ShapeDtypeStruct((t, hq * LANES), BF16),
                   jax.ShapeDtypeStruct((t, hq * LANES), F32)],
        compiler_params=_cparams(("parallel", "parallel")),
        name="gdn_prep",
    )(-jnp.exp(a_log), dt_bias, p, p, p, p, p, p, conv_w, conv_w, conv_w, ab, ab.T)

    gate_blk = (2 * hq * GDN_DK + hq * GDN_DV) // (hq * GDN_DV)
    return pl.pallas_call(
        _gdn_scan_kernel,
        grid=(t // bt,),
        in_specs=[pl.BlockSpec((bt, hq * GDN_DV), lambda i: (i, 0)),
                  pl.BlockSpec((bt, hq * GDN_DK), lambda i: (i, 0)),
                  pl.BlockSpec((bt, hq * GDN_DK), lambda i: (i, 0)),
                  pl.BlockSpec((bt, hq * GDN_DK), lambda i: (i, 0)),
                  pl.BlockSpec((bt, hq * LANES), lambda i: (i, 0)),
                  pl.BlockSpec((bt, hq * LANES), lambda i: (i, 0)),
                  pl.BlockSpec((bt, hq * GDN_DV), lambda i: (i, gate_blk)),
                  pl.BlockSpec((1, GDN_DV), lambda i: (0, 0))],
        out_specs=pl.BlockSpec((bt, hq * GDN_DV), lambda i: (i, 0)),
        out_shape=jax.ShapeDtypeStruct((t, hq * GDN_DV), BF16),
        scratch_shapes=[pltpu.VMEM((hq, GDN_DK, GDN_DV), F32)],
        compiler_params=_cparams(("arbitrary",)),
        name="gdn_scan",
    )(w_v, k_cum, q_g, k_d, attn, eg, p, norm_w.reshape(1, GDN_DV))


def _router_kernel(x_ref, w_ref, b_ref, idx_ref, prob_ref, pre_ref, cnt_ref, run_ref):
    @pl.when(pl.program_id(0) == 0)
    def _():
        run_ref[...] = jnp.zeros_like(run_ref)

    x = x_ref[...]
    w = w_ref[...]
    x_hi = x.astype(BF16)
    x_lo = (x - x_hi.astype(F32)).astype(BF16)
    w_hi = w.astype(BF16)
    w_lo = (w - w_hi.astype(F32)).astype(BF16)
    logits = _dot(x_hi, w_hi) + _dot(x_lo, w_hi) + _dot(x_hi, w_lo) + b_ref[...]
    tm = logits.shape[0]
    lane = lax.broadcasted_iota(I32, logits.shape, 1)
    work = logits
    vals, sels, hots = [], [], []
    for _ in range(TOP_K):
        m = jnp.max(work, axis=1, keepdims=True)
        sel = jnp.min(jnp.where(work == m, lane, N_EXPERTS), axis=1, keepdims=True)
        hot = lane == sel
        vals.append(m)
        sels.append(sel)
        hots.append(hot)
        work = jnp.where(hot, NEG_BIG, work)
    exps = [jnp.exp(vk - vals[0]) for vk in vals]
    denom = exps[0] + exps[1] + exps[2] + exps[3]
    cnt = jnp.zeros(logits.shape, F32)
    for hot in hots:
        cnt = cnt + hot.astype(F32)
    ri = lax.broadcasted_iota(I32, (tm, tm), 0)
    rj = lax.broadcasted_iota(I32, (tm, tm), 1)
    tri = jnp.where(ri > rj, 1.0, 0.0).astype(BF16)
    pexcl = _dot(tri, cnt.astype(BF16)) + run_ref[...]
    lane_o = lax.broadcasted_iota(I32, (tm, LANES), 1)
    idx_o = jnp.zeros((tm, LANES), I32)
    prob_o = jnp.zeros((tm, LANES), F32)
    pre_o = jnp.zeros((tm, LANES), I32)
    for kk in range(TOP_K):
        pre_k = jnp.sum(jnp.where(hots[kk], pexcl, 0.0), axis=1, keepdims=True).astype(I32)
        idx_o = jnp.where(lane_o == kk, sels[kk], idx_o)
        prob_o = jnp.where(lane_o == kk, exps[kk] / denom, prob_o)
        pre_o = jnp.where(lane_o == kk, pre_k, pre_o)
    idx_ref[...] = idx_o
    prob_ref[...] = prob_o
    pre_ref[...] = pre_o
    run_ref[...] = run_ref[...] + jnp.sum(cnt, axis=0, keepdims=True)
    cnt_ref[...] = run_ref[...]


def _router(x, w, b, tm=256):
    t, dm = x.shape
    tm = min(tm, t)
    return pl.pallas_call(
        _router_kernel,
        grid=(t // tm,),
        in_specs=[pl.BlockSpec((tm, dm), lambda i: (i, 0)),
                  pl.BlockSpec((dm, N_EXPERTS), lambda i: (0, 0)),
                  pl.BlockSpec((1, N_EXPERTS), lambda i: (0, 0))],
        out_specs=[pl.BlockSpec((tm, LANES), lambda i: (i, 0)),
                   pl.BlockSpec((tm, LANES), lambda i: (i, 0)),
                   pl.BlockSpec((tm, LANES), lambda i: (i, 0)),
                   pl.BlockSpec((1, N_EXPERTS), lambda i: (0, 0))],
        out_shape=[jax.ShapeDtypeStruct((t, LANES), I32),
                   jax.ShapeDtypeStruct((t, LANES), F32),
                   jax.ShapeDtypeStruct((t, LANES), I32),
                   jax.ShapeDtypeStruct((1, N_EXPERTS), F32)],
        scratch_shapes=[pltpu.VMEM((1, N_EXPERTS), F32)],
        compiler_params=_cparams(("arbitrary",)),
        name="moe_router",
    )(x, w, b.reshape(1, N_EXPERTS))


def _row_copies(n, make_copy, wait):
    def body(jj, carry):
        for u in range(DMA_UNROLL):
            cp = make_copy(jj * DMA_UNROLL + u, jj * (DMA_UNROLL // TOP_K) + u // TOP_K, u % TOP_K)
            if wait:
                cp.wait()
            else:
                cp.start()
        return carry

    lax.fori_loop(0, n // DMA_UNROLL, body, 0)


def _dispatch_kernel(x_ref, dest_ref, xs_in_ref, xs_ref, sem):
    del xs_in_ref

    def row_copy(j, token, choice):
        del choice
        return pltpu.make_async_copy(x_ref.at[pl.ds(token, 1)], xs_ref.at[pl.ds(dest_ref[j], 1)], sem)

    _row_copies(dest_ref.shape[0], row_copy, wait=False)
    _row_copies(dest_ref.shape[0], row_copy, wait=True)


def _dispatch(x, dest_flat, xs_prev):
    t, dm = x.shape
    tb = min(MOE_TB, t)
    return pl.pallas_call(
        _dispatch_kernel,
        grid=(t // tb,),
        in_specs=[pl.BlockSpec((tb, dm), lambda i: (i, 0)),
                  pl.BlockSpec((tb * TOP_K,), lambda i: (i,), memory_space=pltpu.SMEM),
                  pl.BlockSpec(memory_space=pl.ANY)],
        out_specs=pl.BlockSpec(memory_space=pl.ANY),
        out_shape=jax.ShapeDtypeStruct(xs_prev.shape, xs_prev.dtype),
        scratch_shapes=[pltpu.SemaphoreType.DMA(())],
        input_output_aliases={2: 0},
        compiler_params=_cparams(("arbitrary",)),
        name="moe_dispatch",
    )(x, dest_flat, xs_prev)


def _expert_kernel(bexp_ref, nvalid_ref, x_ref, wg_ref, wu_ref, wd_ref, bg_ref, bu_ref, bd_ref, y_ref,
                   wg_s, wu_s, wd_s):
    i = pl.program_id(0)
    valid = i < nvalid_ref[0]
    changed = jnp.logical_or(i == 0, bexp_ref[i] != bexp_ref[jnp.maximum(i - 1, 0)])

    @pl.when(jnp.logical_and(valid, changed))
    def _():
        wg_s[...] = wg_ref[...].astype(BF16)
        wu_s[...] = wu_ref[...].astype(BF16)
        wd_s[...] = wd_ref[...].astype(BF16)

    @pl.when(valid)
    def _():
        half = x_ref.shape[1]
        x_lo, x_hi = _unpack_bf16_pair(x_ref[...])
        x_lo, x_hi = x_lo.astype(BF16), x_hi.astype(BF16)
        glin = _dot(x_lo, wg_s[:half, :]) + _dot(x_hi, wg_s[half:, :]) + bg_ref[...]
        ulin = _dot(x_lo, wu_s[:half, :]) + _dot(x_hi, wu_s[half:, :]) + bu_ref[...]
        glin = jnp.minimum(glin, SWIGLU_LIMIT)
        ulin = jnp.clip(ulin, -SWIGLU_LIMIT, SWIGLU_LIMIT)
        act = ((ulin + 1.0) * glin * _sigmoid(SWIGLU_ALPHA * glin)).astype(BF16)
        y_lo = _dot(act, wd_s[:, :half]) + bd_ref[:, :half]
        y_hi = _dot(act, wd_s[:, half:]) + bd_ref[:, half:]
        y_ref[...] = _pack_bf16_pair(y_lo, y_hi)

    @pl.when(jnp.logical_not(valid))
    def _():
        y_ref[...] = jnp.zeros_like(y_ref)


def _experts(xs, bexp, nvalid, layer, w_gate, b_gate, w_up, b_up, w_down, b_down):
    nrows, half = xs.shape
    dm = 2 * half
    tm = MOE_TM
    nb = nrows // tm
    ff = w_gate.shape[-1]

    def row_map(i, be, nv):
        return (jnp.maximum(jnp.minimum(i, nv[0] - 1), 0), 0)

    def w_map(i, be, nv):
        return (layer, be[i], 0, 0)

    grid_spec = pltpu.PrefetchScalarGridSpec(
        num_scalar_prefetch=2,
        grid=(nb,),
        in_specs=[pl.BlockSpec((tm, half), row_map),
                  pl.BlockSpec((None, None, dm, ff), w_map, pipeline_mode=pl.Buffered(1)),
                  pl.BlockSpec((None, None, dm, ff), w_map, pipeline_mode=pl.Buffered(1)),
                  pl.BlockSpec((None, None, ff, dm), w_map, pipeline_mode=pl.Buffered(1)),
                  pl.BlockSpec((None, None, 1, ff), w_map),
                  pl.BlockSpec((None, None, 1, ff), w_map),
                  pl.BlockSpec((None, None, 1, dm), w_map)],
        out_specs=pl.BlockSpec((tm, half), lambda i, be, nv: (i, 0)),
        scratch_shapes=[pltpu.VMEM((dm, ff), BF16), pltpu.VMEM((dm, ff), BF16),
                        pltpu.VMEM((ff, dm), BF16)])
    nl = w_gate.shape[0]
    return pl.pallas_call(
        _expert_kernel,
        grid_spec=grid_spec,
        out_shape=jax.ShapeDtypeStruct((nrows, half), U32),
        compiler_params=_cparams(("arbitrary",)),
        name="moe_experts",
    )(bexp, nvalid, xs, w_gate, w_up, w_down, b_gate.reshape(nl, N_EXPERTS, 1, ff),
      b_up.reshape(nl, N_EXPERTS, 1, ff), b_down.reshape(nl, N_EXPERTS, 1, dm))


def _combine_kernel(h_ref, prob_ref, dest_ref, dnext_ref, ys_ref, g_ref, b_ref, o_ref, obf_ref, ybuf, sems):
    i = pl.program_id(0)
    nsteps = pl.num_programs(0)
    cur = lax.rem(i, 2)
    n = dest_ref.shape[0]

    def gather(dref, buf):
        def row_copy(j, token, choice):
            return pltpu.make_async_copy(ys_ref.at[pl.ds(dref[j], 1)],
                                         ybuf.at[buf, choice, pl.ds(token, 1)], sems.at[buf])
        return row_copy

    @pl.when(i == 0)
    def _():
        _row_copies(n, gather(dest_ref, cur), wait=False)

    @pl.when(i + 1 < nsteps)
    def _():
        _row_copies(n, gather(dnext_ref, 1 - cur), wait=False)

    _row_copies(n, gather(dest_ref, cur), wait=True)
    half = ybuf.shape[-1]
    prob = prob_ref[...]
    ff_lo, ff_hi = None, None
    for kk in range(TOP_K):
        lo, hi = _unpack_bf16_pair(ybuf[cur, kk])
        pk = prob[:, kk:kk + 1]
        ff_lo = pk * lo if ff_lo is None else ff_lo + pk * lo
        ff_hi = pk * hi if ff_hi is None else ff_hi + pk * hi
    z_lo = DEEPNORM_ALPHA * h_ref[:, :half] + ff_lo
    z_hi = DEEPNORM_ALPHA * h_ref[:, half:] + ff_hi
    inv_n = 1.0 / (2 * half)
    mu = (jnp.sum(z_lo, axis=-1, keepdims=True) + jnp.sum(z_hi, axis=-1, keepdims=True)) * inv_n
    z_lo = z_lo - mu
    z_hi = z_hi - mu
    var = (jnp.sum(z_lo * z_lo, axis=-1, keepdims=True)
           + jnp.sum(z_hi * z_hi, axis=-1, keepdims=True)) * inv_n
    rs = lax.rsqrt(var + LN_EPS)
    y_lo = z_lo * rs * g_ref[:, :half] + b_ref[:, :half]
    y_hi = z_hi * rs * g_ref[:, half:] + b_ref[:, half:]
    o_ref[:, :half] = y_lo
    o_ref[:, half:] = y_hi
    obf_ref[:, :half] = y_lo.astype(BF16)
    obf_ref[:, half:] = y_hi.astype(BF16)


def _combine(h, prob, dest_flat, ys, g, b):
    t, dm = h.shape
    tb = min(MOE_TB, t)
    nsteps = t // tb
    return pl.pallas_call(
        _combine_kernel,
        grid=(nsteps,),
        in_specs=[pl.BlockSpec((tb, dm), lambda i: (i, 0)),
                  pl.BlockSpec((tb, LANES), lambda i: (i, 0)),
                  pl.BlockSpec((tb * TOP_K,), lambda i: (i,), memory_space=pltpu.SMEM),
                  pl.BlockSpec((tb * TOP_K,), lambda i: (jnp.minimum(i + 1, nsteps - 1),),
                               memory_space=pltpu.SMEM),
                  pl.BlockSpec(memory_space=pl.ANY),
                  pl.BlockSpec((1, dm), lambda i: (0, 0)),
                  pl.BlockSpec((1, dm), lambda i: (0, 0))],
        out_specs=[pl.BlockSpec((tb, dm), lambda i: (i, 0)),
                   pl.BlockSpec((tb, dm), lambda i: (i, 0))],
        out_shape=[jax.ShapeDtypeStruct((t, dm), F32), jax.ShapeDtypeStruct((t, dm), BF16)],
        scratch_shapes=[pltpu.VMEM((2, TOP_K, tb, dm // 2), U32), pltpu.SemaphoreType.DMA((2,))],
        compiler_params=_cparams(("arbitrary",)),
        name="moe_combine",
    )(h, prob, dest_flat, dest_flat, ys, g.reshape(1, dm), b.reshape(1, dm))


def _moe_rows(t):
    return ((t * TOP_K) // MOE_TM + N_EXPERTS) * MOE_TM


def _moe_layer(h, hp, xs_prev, layer, w_router, b_router, w_gate, b_gate, w_up, b_up, w_down, b_down,
               ln_g, ln_b):
    t, dm = h.shape
    tm = MOE_TM
    idx, prob, pre, counts = _router(h, w_router, b_router)
    counts = counts.reshape(N_EXPERTS).astype(I32)
    nblk = (counts + tm - 1) // tm
    blk_end = jnp.cumsum(nblk)
    gstart = ((blk_end - nblk) * tm).astype(I32)
    nvalid = blk_end[-1:].astype(I32)
    nb = _moe_rows(t) // tm
    blk = jnp.minimum(jnp.arange(nb, dtype=I32), nvalid[0] - 1)
    bexp = jnp.sum((blk[:, None] >= blk_end[None, :]).astype(I32), axis=1).astype(I32)
    idx4, pre4 = idx[:, :TOP_K], pre[:, :TOP_K]
    hot = idx4[:, :, None] == jnp.arange(N_EXPERTS, dtype=I32)[None, None, :]
    dest = pre4 + jnp.sum(jnp.where(hot, gstart[None, None, :], 0), axis=-1)
    dest_flat = dest.reshape(t * TOP_K).astype(I32)
    xs = _dispatch(hp, dest_flat, xs_prev)
    ys = _experts(xs, bexp, nvalid, layer, w_gate, b_gate, w_up, b_up, w_down, b_down)
    hn, hnb = _combine(h, prob, dest_flat, ys, ln_g, ln_b)
    return hn, hnb, xs


def _even_layer(h, hb, w_in, w_out, s5_b_re, s5_b_im, s5_c_re, s5_c_im, s5_d, s5_a_re, s5_a_im,
                s5_log_dt, s5_w_glu, s5_b_glu, ln_g, ln_b):
    t = h.shape[0]
    w_in_p = jnp.concatenate([w_in[:, :EVEN_QKVG], _s5_channel_major(w_in[:, EVEN_QKVG:], 1)], axis=1)
    w_glu_p = _s5_channel_major(_s5_channel_major(s5_w_glu, 0), 1)
    b_glu_p = _s5_channel_major(s5_b_glu, 0)
    w_out_p = jnp.concatenate([w_out[:RET_WIDTH], _s5_channel_major(w_out[RET_WIDTH:], 0)], axis=0)
    p = _matmul(hb, w_in_p.astype(BF16), w_in.shape[1], BF16, 2048, 512)
    o_ret = _retention(p, t)
    y = _s5(p[:, EVEN_QKVG:], s5_b_re, s5_b_im, s5_c_re, s5_c_im, s5_d, s5_a_re, s5_a_im, s5_log_dt)
    y = _matmul_glu(y, w_glu_p.astype(BF16), b_glu_p)
    mixed = jnp.concatenate([o_ret, y], axis=-1)
    return _matmul_ln(mixed, w_out_p.astype(BF16), h, ln_g, ln_b)


def _odd_layer(h, hb, w_in, w_out, conv_w, a_log, dt_bias, norm_w, ln_g, ln_b):
    t = h.shape[0]
    w16 = w_in.astype(BF16)
    p = _matmul(hb, w16, GDN_MAIN, BF16, 2048, 512)
    ab = _matmul(hb, w_in[:, GDN_MAIN:].astype(BF16), 2 * GDN_HEADS, F32, 1024, 2 * GDN_HEADS)
    o = _gated_deltanet(p, ab, conv_w, a_log, dt_bias, norm_w, t)
    return _matmul_ln(o, w_out.astype(BF16), h, ln_g, ln_b)


def kernel(x, ev_w_in, ev_w_out, ev_s5_b_re, ev_s5_b_im, ev_s5_c_re, ev_s5_c_im, ev_s5_d, ev_s5_a_re,
           ev_s5_a_im, ev_s5_log_dt, ev_s5_w_glu, ev_s5_b_glu, ev_ln_g, ev_ln_b, od_w_in, od_w_out,
           od_conv_w, od_a_log, od_dt_bias, od_norm_w, od_ln_g, od_ln_b, moe_w_router, moe_b_router,
           moe_w_gate, moe_b_gate, moe_w_up, moe_b_up, moe_w_down, moe_b_down, moe_ln_g, moe_ln_b):
    bsz, seq, dm = x.shape
    outs = []
    xs = jnp.zeros((_moe_rows(seq), dm // 2), U32)
    for bi in range(bsz):
        h = x[bi]
        hb = h.astype(BF16)
        for layer in range(DEPTH):
            p = layer // 2
            if layer % 2 == 0:
                h, hp = _even_layer(h, hb, ev_w_in[p], ev_w_out[p], ev_s5_b_re[p], ev_s5_b_im[p],
                                    ev_s5_c_re[p], ev_s5_c_im[p], ev_s5_d[p], ev_s5_a_re[p],
                                    ev_s5_a_im[p], ev_s5_log_dt[p], ev_s5_w_glu[p], ev_s5_b_glu[p],
                                    ev_ln_g[p], ev_ln_b[p])
            else:
                h, hp = _odd_layer(h, hb, od_w_in[p], od_w_out[p], od_conv_w[p], od_a_log[p],
                                   od_dt_bias[p], od_norm_w[p], od_ln_g[p], od_ln_b[p])
            h, hb, xs = _moe_layer(h, hp, xs, layer, moe_w_router[layer], moe_b_router[layer],
                                   moe_w_gate, moe_b_gate, moe_w_up, moe_b_up, moe_w_down, moe_b_down,
                                   moe_ln_g[layer], moe_ln_b[layer])
        outs.append(h)
    return jnp.stack(outs, axis=0)
```

```python
import functools
import math

import jax
import jax.numpy as jnp
from jax import lax
from jax.experimental import pallas as pl
from jax.experimental.pallas import tpu as pltpu

F32 = jnp.float32
BF16 = jnp.bfloat16
I32 = jnp.int32
U32 = jnp.uint32
HIGHEST = lax.Precision.HIGHEST

D_MODEL = 4096
DEPTH = 4
CHUNK = 64
RET_HEADS = 8
RET_WIDTH = D_MODEL // 2
RET_DV = RET_WIDTH // RET_HEADS
RET_DK = RET_DV // 2
ROPE_BASE = 10000.0
S5_WIDTH = D_MODEL - RET_WIDTH
S5_GROUP = 16
S5_GROUPS = S5_WIDTH // S5_GROUP
S5_STATE = 64
EVEN_QKVG = 2 * RET_HEADS * RET_DK + 2 * RET_WIDTH
GDN_HEADS = 16
GDN_DV = D_MODEL // GDN_HEADS
GDN_DK = GDN_DV // 2
CONV_K = 4
GDN_MAIN = 2 * GDN_HEADS * GDN_DK + 2 * GDN_HEADS * GDN_DV
N_EXPERTS = 32
TOP_K = 4
EXPERT_FF = D_MODEL // 16
SWIGLU_LIMIT = 7.0
SWIGLU_ALPHA = 1.702
DEEPNORM_ALPHA = (2 * DEPTH) ** 0.25
LN_EPS = 1e-5
NORM_EPS = 1e-6

V7X_VMEM_LIMIT_BYTES = 56 * 1024 * 1024
LANES = 128
BF16_SUBLANES = 16
S5_SUB = 16
RET_BLOCK = 256
RET_STEP_HEADS = 2
GDN_BLOCK = 256
GDN_SUB = 256
GDN_PREP_HEADS = 4
MOE_TM = 256
MOE_TB = 128
DMA_UNROLL = 8
NEG_BIG = -3.0e38


def _cparams(sem, vmem=V7X_VMEM_LIMIT_BYTES):
    return pltpu.CompilerParams(dimension_semantics=sem, vmem_limit_bytes=vmem)


def _dot(a, b):
    return jnp.dot(a, b, preferred_element_type=F32)


def _dot_nt(a, b):
    return lax.dot_general(a, b, (((1,), (1,)), ((), ())), preferred_element_type=F32)


def _dot_tn(a, b):
    return lax.dot_general(a, b, (((0,), (0,)), ((), ())), preferred_element_type=F32)


def _dot_hi(a, b):
    return jnp.dot(a, b, preferred_element_type=F32, precision=HIGHEST)


def _sigmoid(x):
    return 1.0 / (1.0 + jnp.exp(-x))


def _mm_kernel(x_ref, w_ref, o_ref):
    o_ref[...] = _dot(x_ref[...].astype(BF16), w_ref[...]).astype(o_ref.dtype)


def _matmul(x, w, n, out_dtype, tm, tn):
    m, k = x.shape
    tm, tn = min(tm, m), min(tn, n)
    return pl.pallas_call(
        _mm_kernel,
        grid=(m // tm, n // tn),
        in_specs=[pl.BlockSpec((tm, k), lambda i, j: (i, 0)),
                  pl.BlockSpec((k, tn), lambda i, j: (0, j))],
        out_specs=pl.BlockSpec((tm, tn), lambda i, j: (i, j)),
        out_shape=jax.ShapeDtypeStruct((m, n), out_dtype),
        compiler_params=_cparams(("parallel", "parallel")),
        name="proj_matmul",
    )(x, w)


def _pack_bf16_pair(lo, hi):
    lo_bits = lax.bitcast_convert_type(lo.astype(BF16).astype(F32), U32)
    hi_bits = lax.bitcast_convert_type(hi.astype(BF16).astype(F32), U32)
    return jnp.right_shift(lo_bits, jnp.uint32(16)) | (hi_bits & jnp.uint32(0xFFFF0000))


def _unpack_bf16_pair(word):
    lo = lax.bitcast_convert_type(jnp.left_shift(word, jnp.uint32(16)), F32)
    hi = lax.bitcast_convert_type(word & jnp.uint32(0xFFFF0000), F32)
    return lo, hi


def _mm_ln_kernel(x_ref, w_ref, res_ref, g_ref, b_ref, o_ref, hp_ref, z_ref, *, nj):
    j = pl.program_id(1)
    z_ref[j] = DEEPNORM_ALPHA * res_ref[...] + _dot(x_ref[...].astype(BF16), w_ref[...])

    @pl.when(j == nj - 1)
    def _():
        tn = z_ref.shape[2]
        n = nj * tn
        tot = z_ref[0]
        for jj in range(1, nj):
            tot = tot + z_ref[jj]
        mu = jnp.sum(tot, axis=-1, keepdims=True) * (1.0 / n)
        sq = jnp.square(z_ref[0] - mu)
        for jj in range(1, nj):
            sq = sq + jnp.square(z_ref[jj] - mu)
        rs = lax.rsqrt(jnp.sum(sq, axis=-1, keepdims=True) * (1.0 / n) + LN_EPS)
        half = nj // 2
        for jj in range(half):
            ca = slice(jj * tn, (jj + 1) * tn)
            cb = slice((jj + half) * tn, (jj + half + 1) * tn)
            ya = (z_ref[jj] - mu) * rs * g_ref[:, ca] + b_ref[:, ca]
            yb = (z_ref[jj + half] - mu) * rs * g_ref[:, cb] + b_ref[:, cb]
            o_ref[:, ca] = ya
            o_ref[:, cb] = yb
            hp_ref[:, ca] = _pack_bf16_pair(ya, yb)


def _matmul_ln(x, w, res, g, b, tm=512, tn=512):
    m, kdim = x.shape
    n = w.shape[1]
    tm, tn = min(tm, m), min(tn, n)
    nj = n // tn
    return pl.pallas_call(
        functools.partial(_mm_ln_kernel, nj=nj),
        grid=(m // tm, nj),
        in_specs=[pl.BlockSpec((tm, kdim), lambda i, j: (i, 0)),
                  pl.BlockSpec((kdim, tn), lambda i, j: (0, j)),
                  pl.BlockSpec((tm, tn), lambda i, j: (i, j)),
                  pl.BlockSpec((1, n), lambda i, j: (0, 0)),
                  pl.BlockSpec((1, n), lambda i, j: (0, 0))],
        out_specs=[pl.BlockSpec((tm, n), lambda i, j: (i, 0)),
                   pl.BlockSpec((tm, n // 2), lambda i, j: (i, 0))],
        out_shape=[jax.ShapeDtypeStruct((m, n), F32), jax.ShapeDtypeStruct((m, n // 2), U32)],
        scratch_shapes=[pltpu.VMEM((nj, tm, tn), F32)],
        compiler_params=_cparams(("parallel", "arbitrary")),
        name="proj_layernorm",
    )(x, w, res, g.reshape(1, n), b.reshape(1, n))


def _mm_glu_kernel(x_ref, yt_ref, w_ref, b_ref, o_ref):
    z = _dot(x_ref[...], w_ref[...]) + b_ref[...]
    o_ref[...] = (yt_ref[...].astype(F32) * _sigmoid(z)).astype(o_ref.dtype)


def _matmul_glu(y, w, b, tm=1024, tn=512):
    m, k = y.shape
    n = w.shape[1]
    tm, tn = min(tm, m), min(tn, n)
    return pl.pallas_call(
        _mm_glu_kernel,
        grid=(m // tm, n // tn),
        in_specs=[pl.BlockSpec((tm, k), lambda i, j: (i, 0)),
                  pl.BlockSpec((tm, tn), lambda i, j: (i, j)),
                  pl.BlockSpec((k, tn), lambda i, j: (0, j)),
                  pl.BlockSpec((1, tn), lambda i, j: (0, j))],
        out_specs=pl.BlockSpec((tm, tn), lambda i, j: (i, j)),
        out_shape=jax.ShapeDtypeStruct((m, n), BF16),
        compiler_params=_cparams(("parallel", "parallel")),
        name="s5_glu",
    )(y, y, w, b.reshape(1, n))


def _ret_kernel(bdec_ref, q_ref, k_ref, v_ref, gate_ref, cos_ref, sin_ref, dmask_ref, qdec_ref,
                kdec_ref, o_ref, s_ref):
    @pl.when(pl.program_id(1) == 0)
    def _():
        s_ref[...] = jnp.zeros_like(s_ref)

    cos = cos_ref[...]
    sin = sin_ref[...]
    half = RET_DK // 2
    for hh in range(RET_STEP_HEADS):
        h = pl.program_id(0) * RET_STEP_HEADS + hh
        ck = slice(hh * RET_DK, (hh + 1) * RET_DK)
        cv = slice(hh * RET_DV, (hh + 1) * RET_DV)
        q = q_ref[:, ck].astype(F32)
        k = k_ref[:, ck].astype(F32)
        qr = q * cos + pltpu.roll(q, half, 1) * sin
        kr = (k * cos + pltpu.roll(k, half, 1) * sin) * (RET_DK ** -0.5)
        v = v_ref[:, cv].astype(BF16)
        scores = _dot_nt(qr.astype(BF16), kr.astype(BF16)) * dmask_ref[hh]
        o = _dot(scores.astype(BF16), v)
        state = s_ref[hh]
        o = o + _dot((qr * qdec_ref[hh]).astype(BF16), state.astype(BF16))
        s_ref[hh] = state * bdec_ref[h] + _dot_tn((kr * kdec_ref[hh]).astype(BF16), v)
        mu = jnp.mean(o, axis=-1, keepdims=True)
        oc = o - mu
        var = jnp.mean(oc * oc, axis=-1, keepdims=True)
        on = oc * lax.rsqrt(var + NORM_EPS)
        gate = gate_ref[:, cv].astype(F32)
        o_ref[:, cv] = (on * (gate * _sigmoid(gate))).astype(o_ref.dtype)


def _retention_tables(t, bt):
    half = RET_DK // 2
    inv_freq = jnp.power(ROPE_BASE, -jnp.arange(half, dtype=F32) / half)
    ang = jnp.arange(t, dtype=F32)[:, None] * inv_freq[None, :]
    cos, sin = jnp.cos(ang), jnp.sin(ang)
    cos2 = jnp.concatenate([cos, cos], axis=-1)
    sin2 = jnp.concatenate([-sin, sin], axis=-1)
    log_gamma = jnp.log1p(-jnp.exp2(-5.0 - jnp.arange(RET_HEADS, dtype=F32)))
    pos = jnp.arange(bt, dtype=F32)
    dist = jnp.abs(pos[:, None] - pos[None, :])
    chunk_id = jnp.arange(bt) // CHUNK
    visible = chunk_id[None, :] <= chunk_id[:, None]
    dmask = jnp.where(visible[None], jnp.exp(log_gamma[:, None, None] * dist[None]), 0.0)
    qdec = jnp.exp(log_gamma[:, None] * (pos + 1.0))
    kdec = jnp.exp(log_gamma[:, None] * (bt - 1.0 - pos))
    qdec = jnp.broadcast_to(qdec[:, :, None], (RET_HEADS, bt, RET_DK))
    kdec = jnp.broadcast_to(kdec[:, :, None], (RET_HEADS, bt, RET_DK))
    bdec = jnp.exp(log_gamma * bt)
    return cos2, sin2, dmask, qdec, kdec, bdec


def _retention(p, t):
    bt = min(RET_BLOCK, t)
    cos2, sin2, dmask, qdec, kdec, bdec = _retention_tables(t, bt)
    hs = RET_STEP_HEADS
    ng = RET_HEADS // hs
    kq = ng
    kv = 2 * RET_HEADS * RET_DK // (hs * RET_DV)
    kg = kv + ng
    grid_spec = pltpu.PrefetchScalarGridSpec(
        num_scalar_prefetch=0,
        grid=(ng, t // bt),
        in_specs=[pl.BlockSpec(memory_space=pltpu.SMEM),
                  pl.BlockSpec((bt, hs * RET_DK), lambda g, i: (i, g)),
                  pl.BlockSpec((bt, hs * RET_DK), lambda g, i: (i, kq + g)),
                  pl.BlockSpec((bt, hs * RET_DV), lambda g, i: (i, kv + g)),
                  pl.BlockSpec((bt, hs * RET_DV), lambda g, i: (i, kg + g)),
                  pl.BlockSpec((bt, RET_DK), lambda g, i: (i, 0)),
                  pl.BlockSpec((bt, RET_DK), lambda g, i: (i, 0)),
                  pl.BlockSpec((hs, bt, bt), lambda g, i: (g, 0, 0)),
                  pl.BlockSpec((hs, bt, RET_DK), lambda g, i: (g, 0, 0)),
                  pl.BlockSpec((hs, bt, RET_DK), lambda g, i: (g, 0, 0))],
        out_specs=pl.BlockSpec((bt, hs * RET_DV), lambda g, i: (i, g)),
        scratch_shapes=[pltpu.VMEM((hs, RET_DK, RET_DV), F32)])
    return pl.pallas_call(
        _ret_kernel,
        grid_spec=grid_spec,
        out_shape=jax.ShapeDtypeStruct((t, RET_WIDTH), BF16),
        compiler_params=_cparams(("parallel", "arbitrary")),
        name="retention",
    )(bdec, p, p, p, p, cos2, sin2, dmask, qdec, kdec)


def _gelu_tanh(x):
    c = math.sqrt(2.0 / math.pi)
    return 0.5 * x * (1.0 + jnp.tanh(c * (x + 0.044715 * (x * x * x))))


def _s5_kernel(u_ref, m_ref, bm_ref, cm_ref, pw_ref, d_ref, y_ref, *, nlev):
    u = u_ref[0].astype(F32)
    ub = u.astype(BF16)
    nc = u.shape[0]
    hstate = _dot(ub, bm_ref[0])
    row = lax.broadcasted_iota(I32, hstate.shape, 0)
    pw = pw_ref[0]
    for lev in range(nlev):
        s = 1 << lev
        hs = jnp.where(row >= s, pltpu.roll(hstate, s, 0), 0.0)
        hstate = (hstate + pw[2 * lev:2 * lev + 1, :] * hs
                  + pw[2 * lev + 1:2 * lev + 2, :] * pltpu.roll(hs, S5_STATE, 1))
    if nc > 1:
        hprev = jnp.where(row >= 1, pltpu.roll(hstate, 1, 0), 0.0)
    else:
        hprev = jnp.zeros_like(hstate)
    y = _dot(ub, m_ref[0]) + _dot(hprev.astype(BF16), cm_ref[0]) + d_ref[0] * u
    y_ref[0] = _gelu_tanh(y).astype(y_ref.dtype)


def _s5_tables(b_re, b_im, c_re, c_im, d, a_re, a_im, log_dt, nc):
    g, p, i = S5_GROUPS, S5_STATE, S5_GROUP
    l = S5_SUB
    dt = jnp.exp(log_dt)[:, None]
    lam_re = jnp.minimum(a_re, -1e-4)
    lam_im = a_im
    mag = jnp.exp(lam_re * dt)
    ab_re = mag * jnp.cos(lam_im * dt)
    ab_im = mag * jnp.sin(lam_im * dt)
    den = jnp.square(lam_re) + jnp.square(lam_im)
    f_re = ((ab_re - 1.0) * lam_re + ab_im * lam_im) / den
    f_im = (ab_im * lam_re - (ab_re - 1.0) * lam_im) / den
    bb_re = f_re[..., None] * b_re - f_im[..., None] * b_im
    bb_im = f_re[..., None] * b_im + f_im[..., None] * b_re

    def apow(n):
        n = jnp.asarray(n, F32)[..., None, None]
        m = jnp.exp(n * (lam_re * dt))
        return m * jnp.cos(n * (lam_im * dt)), m * jnp.sin(n * (lam_im * dt))

    wr, wi = apow(jnp.arange(l))
    xr = wr[..., None] * bb_re[None] - wi[..., None] * bb_im[None]
    xi = wr[..., None] * bb_im[None] + wi[..., None] * bb_re[None]
    kern = (jnp.einsum('gip,jgpk->gjik', c_re, xr, precision=HIGHEST)
            - jnp.einsum('gip,jgpk->gjik', c_im, xi, precision=HIGHEST))
    lag = jnp.arange(l)[None, :] - jnp.arange(l)[:, None]
    toe = kern[:, jnp.clip(lag, 0, l - 1)]
    toe = jnp.where((lag >= 0)[None, :, :, None, None], toe, 0.0)
    m_mat = toe.transpose(0, 1, 4, 2, 3).reshape(g, l * i, l * i)
    wr, wi = apow(l - 1 - jnp.arange(l))
    br = wr[..., None] * bb_re[None] - wi[..., None] * bb_im[None]
    bi = wr[..., None] * bb_im[None] + wi[..., None] * bb_re[None]
    bm = jnp.concatenate([br, bi], axis=2)
    bm = bm.transpose(1, 0, 3, 2).reshape(g, l * i, 2 * p)
    wr, wi = apow(jnp.arange(l) + 1.0)
    cr = (c_re.transpose(0, 2, 1)[None] * wr[..., None]
          - c_im.transpose(0, 2, 1)[None] * wi[..., None])
    ci = (-c_re.transpose(0, 2, 1)[None] * wi[..., None]
          - c_im.transpose(0, 2, 1)[None] * wr[..., None])
    cm = jnp.concatenate([cr, ci], axis=2)
    cm = cm.transpose(1, 2, 0, 3).reshape(g, 2 * p, l * i)
    nlev = max(1, int(math.ceil(math.log2(nc)))) if nc > 1 else 0
    if nlev:
        wr, wi = apow(float(l) * (2.0 ** jnp.arange(nlev)))
        pw = jnp.stack([jnp.concatenate([wr, wr], axis=-1),
                        jnp.concatenate([-wi, wi], axis=-1)], axis=1)
        pw = pw.reshape(2 * nlev, g, 2 * p).transpose(1, 0, 2)
    else:
        pw = jnp.zeros((g, 2, 2 * p), F32)
    dd = jnp.tile(d.reshape(g, 1, i), (1, l, 1)).reshape(g, 1, l * i)
    return m_mat.astype(BF16), bm.astype(BF16), cm.astype(BF16), pw, dd, nlev


def _s5_channel_major(w, axis):
    shape = w.shape
    w = w.reshape(shape[:axis] + (S5_GROUPS, S5_GROUP) + shape[axis + 1:])
    return jnp.swapaxes(w, axis, axis + 1).reshape(shape)


def _s5(u, b_re, b_im, c_re, c_im, d, a_re, a_im, log_dt):
    t = u.shape[0]
    g, l, i = S5_GROUPS, S5_SUB, S5_GROUP
    nc = t // l
    m_mat, bm, cm, pw, dd, nlev = _s5_tables(b_re, b_im, c_re, c_im, d, a_re, a_im, log_dt, nc)
    ug = u.reshape(t * i, g).T.reshape(g, nc, l * i)
    npw = pw.shape[1]
    y = pl.pallas_call(
        functools.partial(_s5_kernel, nlev=nlev),
        grid=(g,),
        in_specs=[pl.BlockSpec((1, nc, l * i), lambda j: (j, 0, 0)),
                  pl.BlockSpec((1, l * i, l * i), lambda j: (j, 0, 0)),
                  pl.BlockSpec((1, l * i, 2 * S5_STATE), lambda j: (j, 0, 0)),
                  pl.BlockSpec((1, 2 * S5_STATE, l * i), lambda j: (j, 0, 0)),
                  pl.BlockSpec((1, npw, 2 * S5_STATE), lambda j: (j, 0, 0)),
                  pl.BlockSpec((1, 1, l * i), lambda j: (j, 0, 0))],
        out_specs=pl.BlockSpec((1, nc, l * i), lambda j: (j, 0, 0)),
        out_shape=jax.ShapeDtypeStruct((g, nc, l * i), BF16),
        compiler_params=_cparams(("parallel",)),
        name="s5_scan",
    )(ug, m_mat, bm, cm, pw, dd)
    return y.reshape(g, t * i).T.reshape(t, g * i)


def _causal_conv_silu(x, halo, w):
    row = lax.broadcasted_iota(I32, x.shape, 0)
    y = w[CONV_K - 1:CONV_K, :] * x
    for j in range(1, CONV_K):
        head = jnp.concatenate([pltpu.roll(halo, j, 0)] * (x.shape[0] // halo.shape[0]), axis=0)
        shifted = jnp.where(row >= j, pltpu.roll(x, j, 0), head)
        y = y + w[CONV_K - 1 - j:CONV_K - j, :] * shifted
    return y * _sigmoid(y)


def _softplus(x):
    return jnp.maximum(x, 0.0) + jnp.log(1.0 + jnp.exp(-jnp.abs(x)))


def _gdn_prep_kernel(nega_ref, dtb_ref, q_ref, k_ref, v_ref, qh_ref, kh_ref, vh_ref, wq_ref, wk_ref,
                     wv_ref, ab_ref, abt_ref, wvo_ref, kco_ref, qgo_ref, kdo_ref, ato_ref, ego_ref):
    hp = pl.program_id(0)
    keep = (pl.program_id(1) > 0).astype(F32)
    q_all = _causal_conv_silu(q_ref[...].astype(F32), qh_ref[...].astype(F32) * keep, wq_ref[...])
    k_all = _causal_conv_silu(k_ref[...].astype(F32), kh_ref[...].astype(F32) * keep, wk_ref[...])
    v_all = _causal_conv_silu(v_ref[...].astype(F32), vh_ref[...].astype(F32) * keep, wv_ref[...])
    bt = q_all.shape[0]
    ab = ab_ref[...]
    lane = lax.broadcasted_iota(I32, ab.shape, 1)
    abt = abt_ref[...]
    sub = lax.broadcasted_iota(I32, abt.shape, 0)

    sb = GDN_SUB
    bi = lax.broadcasted_iota(I32, (sb, sb), 0)
    bj = lax.broadcasted_iota(I32, (sb, sb), 1)
    same = (bi // CHUNK) == (bj // CHUNK)
    lower = jnp.logical_and(same, bi >= bj)
    strict = jnp.logical_and(same, bi > bj)
    upper = jnp.logical_and(same, bi <= bj)
    eye = (bi == bj).astype(F32)

    for hh in range(GDN_PREP_HEADS):
        h = hp * GDN_PREP_HEADS + hh
        q_h = q_all[:, hh * GDN_DK:(hh + 1) * GDN_DK]
        k_h = k_all[:, hh * GDN_DK:(hh + 1) * GDN_DK]
        v_h = v_all[:, hh * GDN_DV:(hh + 1) * GDN_DV]
        q_h = q_h * lax.rsqrt(jnp.sum(q_h * q_h, axis=-1, keepdims=True) + NORM_EPS) * (GDN_DK ** -0.5)
        k_h = k_h * lax.rsqrt(jnp.sum(k_h * k_h, axis=-1, keepdims=True) + NORM_EPS)
        neg_a = nega_ref[h]
        dtb = dtb_ref[h]
        a_col = jnp.sum(jnp.where(lane == h, ab, 0.0), axis=1, keepdims=True)
        b_col = jnp.sum(jnp.where(lane == GDN_HEADS + h, ab, 0.0), axis=1, keepdims=True)
        a_row = jnp.sum(jnp.where(sub == h, abt, 0.0), axis=0, keepdims=True)
        g_col_h = neg_a * _softplus(a_col + dtb)
        g_row_h = neg_a * _softplus(a_row + dtb)
        beta_h = _sigmoid(b_col)
        for si in range(bt // sb):
            rs = slice(si * sb, (si + 1) * sb)
            q, k, v = q_h[rs], k_h[rs], v_h[rs]
            g_col, g_row, beta = g_col_h[rs], g_row_h[:, rs], beta_h[rs]
            gcum_col = jnp.sum(jnp.where(lower, g_row, 0.0), axis=1, keepdims=True)
            gcum_row = jnp.sum(jnp.where(upper, g_col, 0.0), axis=0, keepdims=True)
            gtot_col = jnp.sum(jnp.where(same, g_row, 0.0), axis=1, keepdims=True)
            decay = jnp.where(lower, jnp.exp(jnp.where(lower, gcum_col - gcum_row, 0.0)), 0.0)
            k_beta = k * beta
            v_beta = v * beta
            k16 = k.astype(BF16)
            lmat = jnp.where(strict, _dot_nt(k_beta.astype(BF16), k16) * decay, 0.0)
            tinv = eye - lmat
            l16 = lmat.astype(BF16)
            pw = _dot(l16, l16)
            n = 2
            while True:
                tinv = tinv + _dot(tinv.astype(BF16), pw.astype(BF16))
                n *= 2
                if n >= CHUNK:
                    break
                p16 = pw.astype(BF16)
                pw = _dot(p16, p16)
            t16 = tinv.astype(BF16)
            eg = jnp.exp(gcum_col)
            w_v = _dot(t16, v_beta.astype(BF16))
            k_cum = _dot(t16, (k_beta * eg).astype(BF16))
            attn = jnp.where(lower, _dot_nt(q.astype(BF16), k16) * decay, 0.0)
            folded = attn[:, :LANES]
            for c in range(1, sb // LANES):
                folded = folded + attn[:, c * LANES:(c + 1) * LANES]
            folded = folded + pltpu.roll(folded, CHUNK, 1)
            wvo_ref[rs, hh * GDN_DV:(hh + 1) * GDN_DV] = w_v.astype(BF16)
            kco_ref[rs, hh * GDN_DK:(hh + 1) * GDN_DK] = k_cum.astype(BF16)
            qgo_ref[rs, hh * GDN_DK:(hh + 1) * GDN_DK] = (q * eg).astype(BF16)
            kdo_ref[rs, hh * GDN_DK:(hh + 1) * GDN_DK] = (k * jnp.exp(gtot_col - gcum_col)).astype(BF16)
            ato_ref[rs, hh * LANES:(hh + 1) * LANES] = folded.astype(BF16)
            ego_ref[rs, hh * LANES:(hh + 1) * LANES] = jnp.broadcast_to(jnp.exp(gtot_col), (sb, LANES))


def _gdn_scan_kernel(wv_ref, kc_ref, qg_ref, kd_ref, at_ref, eg_ref, gate_ref, nw_ref, o_ref, s_ref):
    @pl.when(pl.program_id(0) == 0)
    def _():
        s_ref[...] = jnp.zeros_like(s_ref)

    nw = nw_ref[...]

    def chunk_step(c, carry):
        r0 = pl.multiple_of(c * CHUNK, CHUNK)
        rows = pl.ds(r0, CHUNK)
        for h in range(GDN_HEADS):
            ck = slice(h * GDN_DK, (h + 1) * GDN_DK)
            cv = slice(h * GDN_DV, (h + 1) * GDN_DV)
            state = s_ref[h]
            s16 = state.astype(BF16)
            ks = _dot(jnp.concatenate([kc_ref[rows, ck], qg_ref[rows, ck]], axis=0), s16)
            v_new = wv_ref[rows, cv].astype(F32) - ks[:CHUNK]
            vn16 = v_new.astype(BF16)
            attn = at_ref[rows, h * LANES:h * LANES + CHUNK]
            o = ks[CHUNK:] + _dot(attn, vn16)
            eg = eg_ref[pl.ds(r0, 1), h * LANES:h * LANES + 1]
            s_ref[h] = state * eg + _dot_tn(kd_ref[rows, ck], vn16)
            rms = lax.rsqrt(jnp.mean(o * o, axis=-1, keepdims=True) + NORM_EPS)
            gc = gate_ref[rows, cv].astype(F32)
            o_ref[rows, cv] = (o * rms * nw * (gc * _sigmoid(gc))).astype(o_ref.dtype)
        return carry

    lax.fori_loop(0, wv_ref.shape[0] // CHUNK, chunk_step, 0)


def _gated_deltanet(p, ab, conv_w, a_log, dt_bias, norm_w, t):
    bt = min(GDN_BLOCK, t)
    hq, hg = GDN_HEADS, GDN_PREP_HEADS
    wk_, wv_ = hg * GDN_DK, hg * GDN_DV
    kk = hq * GDN_DK // wk_
    kv = 2 * hq * GDN_DK // wv_
    hb = bt // BF16_SUBLANES

    def halo_map(off):
        return lambda g, i: (jnp.maximum(i * hb - 1, 0), off + g)

    prep_spec = pltpu.PrefetchScalarGridSpec(
        num_scalar_prefetch=0,
        grid=(hq // hg, t // bt),
        in_specs=[pl.BlockSpec(memory_space=pltpu.SMEM),
                  pl.BlockSpec(memory_space=pltpu.SMEM),
                  pl.BlockSpec((bt, wk_), lambda g, i: (i, g)),
                  pl.BlockSpec((bt, wk_), lambda g, i: (i, kk + g)),
                  pl.BlockSpec((bt, wv_), lambda g, i: (i, kv + g)),
                  pl.BlockSpec((BF16_SUBLANES, wk_), halo_map(0)),
                  pl.BlockSpec((BF16_SUBLANES, wk_), halo_map(kk)),
                  pl.BlockSpec((BF16_SUBLANES, wv_), halo_map(kv)),
                  pl.BlockSpec((CONV_K, wk_), lambda g, i: (0, g)),
                  pl.BlockSpec((CONV_K, wk_), lambda g, i: (0, kk + g)),
                  pl.BlockSpec((CONV_K, wv_), lambda g, i: (0, kv + g)),
                  pl.BlockSpec((bt, 2 * hq), lambda g, i: (i, 0)),
                  pl.BlockSpec((2 * hq, bt), lambda g, i: (0, i))],
        out_specs=[pl.BlockSpec((bt, wv_), lambda g, i: (i, g)),
                   pl.BlockSpec((bt, wk_), lambda g, i: (i, g)),
                   pl.BlockSpec((bt, wk_), lambda g, i: (i, g)),
                   pl.BlockSpec((bt, wk_), lambda g, i: (i, g)),
                   pl.BlockSpec((bt, hg * LANES), lambda g, i: (i, g)),
                   pl.BlockSpec((bt, hg * LANES), lambda g, i: (i, g))])
    w_v, k_cum, q_g, k_d, attn, eg = pl.pallas_call(
        _gdn_prep_kernel,
        grid_spec=prep_spec,
        out_shape=[jax.ShapeDtypeStruct((t, hq * GDN_DV), BF16),
                   jax.ShapeDtypeStruct((t, hq * GDN_DK), BF16),
                   jax.ShapeDtypeStruct((t, hq * GDN_DK), BF16),
                   jax.ShapeDtypeStruct((t, hq * GDN_DK), BF16),
                   jax.ShapeDtypeStruct((t, hq * LANES), BF16),
                   jax.ShapeDtypeStruct((t, hq * LANES), F32)],
        compiler_params=_cparams(("parallel", "parallel")),
        name="gdn_prep",
    )(-jnp.exp(a_log), dt_bias, p, p, p, p, p, p, conv_w, conv_w, conv_w, ab, ab.T)

    gate_blk = (2 * hq * GDN_DK + hq * GDN_DV) // (hq * GDN_DV)
    return pl.pallas_call(
        _gdn_scan_kernel,
        grid=(t // bt,),
        in_specs=[pl.BlockSpec((bt, hq * GDN_DV), lambda i: (i, 0)),
                  pl.BlockSpec((bt, hq * GDN_DK), lambda i: (i, 0)),
                  pl.BlockSpec((bt, hq * GDN_DK), lambda i: (i, 0)),
                  pl.BlockSpec((bt, hq * GDN_DK), lambda i: (i, 0)),
                  pl.BlockSpec((bt, hq * LANES), lambda i: (i, 0)),
                  pl.BlockSpec((bt, hq * LANES), lambda i: (i, 0)),
                  pl.BlockSpec((bt, hq * GDN_DV), lambda i: (i, gate_blk)),
                  pl.BlockSpec((1, GDN_DV), lambda i: (0, 0))],
        out_specs=pl.BlockSpec((bt, hq * GDN_DV), lambda i: (i, 0)),
        out_shape=jax.ShapeDtypeStruct((t, hq * GDN_DV), BF16),
        scratch_shapes=[pltpu.VMEM((hq, GDN_DK, GDN_DV), F32)],
        compiler_params=_cparams(("arbitrary",)),
        name="gdn_scan",
    )(w_v, k_cum, q_g, k_d, attn, eg, p, norm_w.reshape(1, GDN_DV))


def _router_kernel(x_ref, w_ref, b_ref, idx_ref, prob_ref, pre_ref, cnt_ref, run_ref):
    @pl.when(pl.program_id(0) == 0)
    def _():
        run_ref[...] = jnp.zeros_like(run_ref)

    x = x_ref[...]
    w = w_ref[...]
    x_hi = x.astype(BF16)
    x_lo = (x - x_hi.astype(F32)).astype(BF16)
    w_hi = w.astype(BF16)
    w_lo = (w - w_hi.astype(F32)).astype(BF16)
    logits = _dot(x_hi, w_hi) + _dot(x_lo, w_hi) + _dot(x_hi, w_lo) + b_ref[...]
    tm = logits.shape[0]
    lane = lax.broadcasted_iota(I32, logits.shape, 1)
    work = logits
    vals, sels, hots = [], [], []
    for _ in range(TOP_K):
        m = jnp.max(work, axis=1, keepdims=True)
        sel = jnp.min(jnp.where(work == m, lane, N_EXPERTS), axis=1, keepdims=True)
        hot = lane == sel
        vals.append(m)
        sels.append(sel)
        hots.append(hot)
        work = jnp.where(hot, NEG_BIG, work)
    exps = [jnp.exp(vk - vals[0]) for vk in vals]
    denom = exps[0] + exps[1] + exps[2] + exps[3]
    cnt = jnp.zeros(logits.shape, F32)
    for hot in hots:
        cnt = cnt + hot.astype(F32)
    ri = lax.broadcasted_iota(I32, (tm, tm), 0)
    rj = lax.broadcasted_iota(I32, (tm, tm), 1)
    tri = jnp.where(ri > rj, 1.0, 0.0).astype(BF16)
    pexcl = _dot(tri, cnt.astype(BF16)) + run_ref[...]
    lane_o = lax.broadcasted_iota(I32, (tm, LANES), 1)
    idx_o = jnp.zeros((tm, LANES), I32)
    prob_o = jnp.zeros((tm, LANES), F32)
    pre_o = jnp.zeros((tm, LANES), I32)
    for kk in range(TOP_K):
        pre_k = jnp.sum(jnp.where(hots[kk], pexcl, 0.0), axis=1, keepdims=True).astype(I32)
        idx_o = jnp.where(lane_o == kk, sels[kk], idx_o)
        prob_o = jnp.where(lane_o == kk, exps[kk] / denom, prob_o)
        pre_o = jnp.where(lane_o == kk, pre_k, pre_o)
    idx_ref[...] = idx_o
    prob_ref[...] = prob_o
    pre_ref[...] = pre_o
    run_ref[...] = run_ref[...] + jnp.sum(cnt, axis=0, keepdims=True)
    cnt_ref[...] = run_ref[...]


def _router(x, w, b, tm=256):
    t, dm = x.shape
    tm = min(tm, t)
    return pl.pallas_call(
        _router_kernel,
        grid=(t // tm,),
        in_specs=[pl.BlockSpec((tm, dm), lambda i: (i, 0)),
                  pl.BlockSpec((dm, N_EXPERTS), lambda i: (0, 0)),
                  pl.BlockSpec((1, N_EXPERTS), lambda i: (0, 0))],
        out_specs=[pl.BlockSpec((tm, LANES), lambda i: (i, 0)),
                   pl.BlockSpec((tm, LANES), lambda i: (i, 0)),
                   pl.BlockSpec((tm, LANES), lambda i: (i, 0)),
                   pl.BlockSpec((1, N_EXPERTS), lambda i: (0, 0))],
        out_shape=[jax.ShapeDtypeStruct((t, LANES), I32),
                   jax.ShapeDtypeStruct((t, LANES), F32),
                   jax.ShapeDtypeStruct((t, LANES), I32),
                   jax.ShapeDtypeStruct((1, N_EXPERTS), F32)],
        scratch_shapes=[pltpu.VMEM((1, N_EXPERTS), F32)],
        compiler_params=_cparams(("arbitrary",)),
        name="moe_router",
    )(x, w, b.reshape(1, N_EXPERTS))


def _row_copies(n, make_copy, wait):
    def body(jj, carry):
        for u in range(DMA_UNROLL):
            cp = make_copy(jj * DMA_UNROLL + u, jj * (DMA_UNROLL // TOP_K) + u // TOP_K, u % TOP_K)
            if wait:
                cp.wait()
            else:
                cp.start()
        return carry

    lax.fori_loop(0, n // DMA_UNROLL, body, 0)


def _dispatch_kernel(x_ref, dest_ref, xs_in_ref, xs_ref, sem):
    del xs_in_ref

    def row_copy(j, token, choice):
        del choice
        return pltpu.make_async_copy(x_ref.at[pl.ds(token, 1)], xs_ref.at[pl.ds(dest_ref[j], 1)], sem)

    _row_copies(dest_ref.shape[0], row_copy, wait=False)
    _row_copies(dest_ref.shape[0], row_copy, wait=True)


def _dispatch(x, dest_flat, xs_prev):
    t, dm = x.shape
    tb = min(MOE_TB, t)
    return pl.pallas_call(
        _dispatch_kernel,
        grid=(t // tb,),
        in_specs=[pl.BlockSpec((tb, dm), lambda i: (i, 0)),
                  pl.BlockSpec((tb * TOP_K,), lambda i: (i,), memory_space=pltpu.SMEM),
                  pl.BlockSpec(memory_space=pl.ANY)],
        out_specs=pl.BlockSpec(memory_space=pl.ANY),
        out_shape=jax.ShapeDtypeStruct(xs_prev.shape, xs_prev.dtype),
        scratch_shapes=[pltpu.SemaphoreType.DMA(())],
        input_output_aliases={2: 0},
        compiler_params=_cparams(("arbitrary",)),
        name="moe_dispatch",
    )(x, dest_flat, xs_prev)


def _expert_kernel(bexp_ref, nvalid_ref, x_ref, wg_ref, wu_ref, wd_ref, bg_ref, bu_ref, bd_ref, y_ref,
                   wg_s, wu_s, wd_s):
    i = pl.program_id(0)
    valid = i < nvalid_ref[0]
    changed = jnp.logical_or(i == 0, bexp_ref[i] != bexp_ref[jnp.maximum(i - 1, 0)])

    @pl.when(jnp.logical_and(valid, changed))
    def _():
        wg_s[...] = wg_ref[...].astype(BF16)
        wu_s[...] = wu_ref[...].astype(BF16)
        wd_s[...] = wd_ref[...].astype(BF16)

    @pl.when(valid)
    def _():
        half = x_ref.shape[1]
        x_lo, x_hi = _unpack_bf16_pair(x_ref[...])
        x_lo, x_hi = x_lo.astype(BF16), x_hi.astype(BF16)
        glin = _dot(x_lo, wg_s[:half, :]) + _dot(x_hi, wg_s[half:, :]) + bg_ref[...]
        ulin = _dot(x_lo, wu_s[:half, :]) + _dot(x_hi, wu_s[half:, :]) + bu_ref[...]
        glin = jnp.minimum(glin, SWIGLU_LIMIT)
        ulin = jnp.clip(ulin, -SWIGLU_LIMIT, SWIGLU_LIMIT)
        act = ((ulin + 1.0) * glin * _sigmoid(SWIGLU_ALPHA * glin)).astype(BF16)
        y_lo = _dot(act, wd_s[:, :half]) + bd_ref[:, :half]
        y_hi = _dot(act, wd_s[:, half:]) + bd_ref[:, half:]
        y_ref[...] = _pack_bf16_pair(y_lo, y_hi)

    @pl.when(jnp.logical_not(valid))
    def _():
        y_ref[...] = jnp.zeros_like(y_ref)


def _experts(xs, bexp, nvalid, layer, w_gate, b_gate, w_up, b_up, w_down, b_down):
    nrows, half = xs.shape
    dm = 2 * half
    tm = MOE_TM
    nb = nrows // tm
    ff = w_gate.shape[-1]

    def row_map(i, be, nv):
        return (jnp.maximum(jnp.minimum(i, nv[0] - 1), 0), 0)

    def w_map(i, be, nv):
        return (layer, be[i], 0, 0)

    grid_spec = pltpu.PrefetchScalarGridSpec(
        num_scalar_prefetch=2,
        grid=(nb,),
        in_specs=[pl.BlockSpec((tm, half), row_map),
                  pl.BlockSpec((None, None, dm, ff), w_map),
                  pl.BlockSpec((None, None, dm, ff), w_map),
                  pl.BlockSpec((None, None, ff, dm), w_map),
                  pl.BlockSpec((None, None, 1, ff), w_map),
                  pl.BlockSpec((None, None, 1, ff), w_map),
                  pl.BlockSpec((None, None, 1, dm), w_map)],
        out_specs=pl.BlockSpec((tm, half), lambda i, be, nv: (i, 0)),
        scratch_shapes=[pltpu.VMEM((dm, ff), BF16), pltpu.VMEM((dm, ff), BF16),
                        pltpu.VMEM((ff, dm), BF16)])
    nl = w_gate.shape[0]
    return pl.pallas_call(
        _expert_kernel,
        grid_spec=grid_spec,
        out_shape=jax.ShapeDtypeStruct((nrows, half), U32),
        compiler_params=_cparams(("arbitrary",)),
        name="moe_experts",
    )(bexp, nvalid, xs, w_gate, w_up, w_down, b_gate.reshape(nl, N_EXPERTS, 1, ff),
      b_up.reshape(nl, N_EXPERTS, 1, ff), b_down.reshape(nl, N_EXPERTS, 1, dm))


def _combine_kernel(h_ref, prob_ref, dest_ref, dnext_ref, ys_ref, g_ref, b_ref, o_ref, obf_ref, ybuf, sems):
    i = pl.program_id(0)
    nsteps = pl.num_programs(0)
    cur = lax.rem(i, 2)
    n = dest_ref.shape[0]

    def gather(dref, buf):
        def row_copy(j, token, choice):
            return pltpu.make_async_copy(ys_ref.at[pl.ds(dref[j], 1)],
                                         ybuf.at[buf, choice, pl.ds(token, 1)], sems.at[buf])
        return row_copy

    @pl.when(i == 0)
    def _():
        _row_copies(n, gather(dest_ref, cur), wait=False)

    @pl.when(i + 1 < nsteps)
    def _():
        _row_copies(n, gather(dnext_ref, 1 - cur), wait=False)

    _row_copies(n, gather(dest_ref, cur), wait=True)
    half = ybuf.shape[-1]
    prob = prob_ref[...]
    ff_lo, ff_hi = None, None
    for kk in range(TOP_K):
        lo, hi = _unpack_bf16_pair(ybuf[cur, kk])
        pk = prob[:, kk:kk + 1]
        ff_lo = pk * lo if ff_lo is None else ff_lo + pk * lo
        ff_hi = pk * hi if ff_hi is None else ff_hi + pk * hi
    z_lo = DEEPNORM_ALPHA * h_ref[:, :half] + ff_lo
    z_hi = DEEPNORM_ALPHA * h_ref[:, half:] + ff_hi
    inv_n = 1.0 / (2 * half)
    mu = (jnp.sum(z_lo, axis=-1, keepdims=True) + jnp.sum(z_hi, axis=-1, keepdims=True)) * inv_n
    z_lo = z_lo - mu
    z_hi = z_hi - mu
    var = (jnp.sum(z_lo * z_lo, axis=-1, keepdims=True)
           + jnp.sum(z_hi * z_hi, axis=-1, keepdims=True)) * inv_n
    rs = lax.rsqrt(var + LN_EPS)
    y_lo = z_lo * rs * g_ref[:, :half] + b_ref[:, :half]
    y_hi = z_hi * rs * g_ref[:, half:] + b_ref[:, half:]
    o_ref[:, :half] = y_lo
    o_ref[:, half:] = y_hi
    obf_ref[:, :half] = y_lo.astype(BF16)
    obf_ref[:, half:] = y_hi.astype(BF16)


def _combine(h, prob, dest_flat, ys, g, b):
    t, dm = h.shape
    tb = min(MOE_TB, t)
    nsteps = t // tb
    return pl.pallas_call(
        _combine_kernel,
        grid=(nsteps,),
        in_specs=[pl.BlockSpec((tb, dm), lambda i: (i, 0)),
                  pl.BlockSpec((tb, LANES), lambda i: (i, 0)),
                  pl.BlockSpec((tb * TOP_K,), lambda i: (i,), memory_space=pltpu.SMEM),
                  pl.BlockSpec((tb * TOP_K,), lambda i: (jnp.minimum(i + 1, nsteps - 1),),
                               memory_space=pltpu.SMEM),
                  pl.BlockSpec(memory_space=pl.ANY),
                  pl.BlockSpec((1, dm), lambda i: (0, 0)),
                  pl.BlockSpec((1, dm), lambda i: (0, 0))],
        out_specs=[pl.BlockSpec((tb, dm), lambda i: (i, 0)),
                   pl.BlockSpec((tb, dm), lambda i: (i, 0))],
        out_shape=[jax.ShapeDtypeStruct((t, dm), F32), jax.ShapeDtypeStruct((t, dm), BF16)],
        scratch_shapes=[pltpu.VMEM((2, TOP_K, tb, dm // 2), U32), pltpu.SemaphoreType.DMA((2,))],
        compiler_params=_cparams(("arbitrary",)),
        name="moe_combine",
    )(h, prob, dest_flat, dest_flat, ys, g.reshape(1, dm), b.reshape(1, dm))


def _moe_rows(t):
    return ((t * TOP_K) // MOE_TM + N_EXPERTS) * MOE_TM


def _moe_layer(h, hp, xs_prev, layer, w_router, b_router, w_gate, b_gate, w_up, b_up, w_down, b_down,
               ln_g, ln_b):
    t, dm = h.shape
    tm = MOE_TM
    idx, prob, pre, counts = _router(h, w_router, b_router)
    counts = counts.reshape(N_EXPERTS).astype(I32)
    nblk = (counts + tm - 1) // tm
    blk_end = jnp.cumsum(nblk)
    gstart = ((blk_end - nblk) * tm).astype(I32)
    nvalid = blk_end[-1:].astype(I32)
    nb = _moe_rows(t) // tm
    blk = jnp.minimum(jnp.arange(nb, dtype=I32), nvalid[0] - 1)
    bexp = jnp.sum((blk[:, None] >= blk_end[None, :]).astype(I32), axis=1).astype(I32)
    idx4, pre4 = idx[:, :TOP_K], pre[:, :TOP_K]
    hot = idx4[:, :, None] == jnp.arange(N_EXPERTS, dtype=I32)[None, None, :]
    dest = pre4 + jnp.sum(jnp.where(hot, gstart[None, None, :], 0), axis=-1)
    dest_flat = dest.reshape(t * TOP_K).astype(I32)
    xs = _dispatch(hp, dest_flat, xs_prev)
    ys = _experts(xs, bexp, nvalid, layer, w_gate, b_gate, w_up, b_up, w_down, b_down)
    hn, hnb = _combine(h, prob, dest_flat, ys, ln_g, ln_b)
    return hn, hnb, xs


def _even_layer(h, hb, w_in, w_out, s5_b_re, s5_b_im, s5_c_re, s5_c_im, s5_d, s5_a_re, s5_a_im,
                s5_log_dt, s5_w_glu, s5_b_glu, ln_g, ln_b):
    t = h.shape[0]
    w_in_p = jnp.concatenate([w_in[:, :EVEN_QKVG], _s5_channel_major(w_in[:, EVEN_QKVG:], 1)], axis=1)
    w_glu_p = _s5_channel_major(_s5_channel_major(s5_w_glu, 0), 1)
    b_glu_p = _s5_channel_major(s5_b_glu, 0)
    w_out_p = jnp.concatenate([w_out[:RET_WIDTH], _s5_channel_major(w_out[RET_WIDTH:], 0)], axis=0)
    p = _matmul(hb, w_in_p.astype(BF16), w_in.shape[1], BF16, 2048, 512)
    o_ret = _retention(p, t)
    y = _s5(p[:, EVEN_QKVG:], s5_b_re, s5_b_im, s5_c_re, s5_c_im, s5_d, s5_a_re, s5_a_im, s5_log_dt)
    y = _matmul_glu(y, w_glu_p.astype(BF16), b_glu_p)
    mixed = jnp.concatenate([o_ret, y], axis=-1)
    return _matmul_ln(mixed, w_out_p.astype(BF16), h, ln_g, ln_b)


def _odd_layer(h, hb, w_in, w_out, conv_w, a_log, dt_bias, norm_w, ln_g, ln_b):
    t = h.shape[0]
    w16 = w_in.astype(BF16)
    p = _matmul(hb, w16, GDN_MAIN, BF16, 2048, 512)
    ab = _matmul(hb, w_in[:, GDN_MAIN:].astype(BF16), 2 * GDN_HEADS, F32, 1024, 2 * GDN_HEADS)
    o = _gated_deltanet(p, ab, conv_w, a_log, dt_bias, norm_w, t)
    return _matmul_ln(o, w_out.astype(BF16), h, ln_g, ln_b)


def kernel(x, ev_w_in, ev_w_out, ev_s5_b_re, ev_s5_b_im, ev_s5_c_re, ev_s5_c_im, ev_s5_d, ev_s5_a_re,
           ev_s5_a_im, ev_s5_log_dt, ev_s5_w_glu, ev_s5_b_glu, ev_ln_g, ev_ln_b, od_w_in, od_w_out,
           od_conv_w, od_a_log, od_dt_bias, od_norm_w, od_ln_g, od_ln_b, moe_w_router, moe_b_router,
           moe_w_gate, moe_b_gate, moe_w_up, moe_b_up, moe_w_down, moe_b_down, moe_ln_g, moe_ln_b):
    bsz, seq, dm = x.shape
    outs = []
    xs = jnp.zeros((_moe_rows(seq), dm // 2), U32)
    for bi in range(bsz):
        h = x[bi]
        hb = h.astype(BF16)
        for layer in range(DEPTH):
            p = layer // 2
            if layer % 2 == 0:
                h, hp = _even_layer(h, hb, ev_w_in[p], ev_w_out[p], ev_s5_b_re[p], ev_s5_b_im[p],
                                    ev_s5_c_re[p], ev_s5_c_im[p], ev_s5_d[p], ev_s5_a_re[p],
                                    ev_s5_a_im[p], ev_s5_log_dt[p], ev_s5_w_glu[p], ev_s5_b_glu[p],
                                    ev_ln_g[p], ev_ln_b[p])
            else:
                h, hp = _odd_layer(h, hb, od_w_in[p], od_w_out[p], od_conv_w[p], od_a_log[p],
                                   od_dt_bias[p], od_norm_w[p], od_ln_g[p], od_ln_b[p])
            h, hb, xs = _moe_layer(h, hp, xs, layer, moe_w_router[layer], moe_b_router[layer],
                                   moe_w_gate, moe_b_gate, moe_w_up, moe_b_up, moe_w_down, moe_b_down,
                                   moe_ln_g[layer], moe_ln_b[layer])
        outs.append(h)
    return jnp.stack(outs, axis=0)
```
